```python
import jax, jax.numpy as jnp
from jax import lax
import numpy as np

D_MODEL = 1024
BATCH = 4
SEQ = 8192
DEPTH = 2

N_META = 16
PAD = 128
BLOCK_Q = 128
EPS = 1e-6
NEG = -1e30

FOX_HEADS = 8
FOX_DH = 64
FOX_WIDTH = FOX_HEADS * FOX_DH
GLA_HEADS = 4
GLA_DK = 64
GLA_DV = 128
GLA_KW = GLA_HEADS * GLA_DK
GLA_WIDTH = GLA_HEADS * GLA_DV
GLA_RANK = 16
GLA_GATE_NORM = 16.0
GLA_CHUNK = 16
LRU_WIDTH = 2 * GLA_WIDTH
LRU_BLOCKS = 8
LRU_BS = LRU_WIDTH // LRU_BLOCKS
CONV_W = 4
LRU_C = 8.0

D_MIX = FOX_WIDTH + GLA_WIDTH + LRU_WIDTH
COL_SIZES = (FOX_WIDTH, FOX_WIDTH, FOX_WIDTH, FOX_HEADS, FOX_WIDTH,
             GLA_KW, GLA_KW, GLA_WIDTH, GLA_RANK, GLA_WIDTH,
             LRU_WIDTH, LRU_WIDTH)
D_IN = 3 * FOX_WIDTH + FOX_HEADS + FOX_WIDTH + 2 * GLA_KW + GLA_WIDTH + GLA_RANK + GLA_WIDTH + 2 * LRU_WIDTH

kernel_name = "hybrid_fox_gla_rglru_parallel_heads"


def rms_norm(x, g):
    xf = x.astype(jnp.float32)
    y = xf * lax.rsqrt(jnp.mean(xf * xf, axis=-1, keepdims=True) + EPS)
    return (y * g.astype(jnp.float32)).astype(x.dtype)


def split_projection(proj):
    offs = []
    acc = 0
    for s in COL_SIZES[:-1]:
        acc += s
        offs.append(acc)
    return jnp.split(proj, offs, axis=-1)


def fox_attention(q, k, v, log_f, valid):
    B, L, H, dh = q.shape
    nb = L // BLOCK_Q
    scale = dh ** -0.5
    c = jnp.cumsum(log_f, axis=1).transpose(0, 2, 1)
    kf = k.astype(jnp.float32).transpose(0, 2, 1, 3)
    vf = v.astype(jnp.float32).transpose(0, 2, 1, 3)
    qb = q.astype(jnp.float32).reshape(B, nb, BLOCK_Q, H, dh).transpose(1, 0, 3, 2, 4)
    cqb = c.reshape(B, H, nb, BLOCK_Q).transpose(2, 0, 1, 3)
    kpos = jnp.arange(L)

    def one_block(args):
        qi, cqi, i = args
        qpos = i * BLOCK_Q + jnp.arange(BLOCK_Q)
        s = jnp.einsum('bhqd,bhkd->bhqk', qi, kf) * scale + cqi[..., None] - c[:, :, None, :]
        mask = (kpos[None, :] <= qpos[:, None]) & valid[None, :]
        p = jax.nn.softmax(jnp.where(mask, s, NEG), axis=-1)
        return jnp.einsum('bhqk,bhkd->bhqd', p, vf)

    o = lax.map(one_block, (qb, cqb, jnp.arange(nb)))
    return o.transpose(1, 0, 3, 2, 4).reshape(B, L, H * dh)


def gla_chunked(q, k, v, log_a):
    B, L, H, dk = q.shape
    dv = v.shape[-1]
    C = GLA_CHUNK
    n = L // C

    def chunks(t):
        return t.astype(jnp.float32).reshape(B, n, C, H, t.shape[-1]).transpose(0, 3, 1, 2, 4)

    qc = chunks(q) * (dk ** -0.5)
    kc, vc, gc = chunks(k), chunks(v), chunks(log_a)
    b = jnp.cumsum(gc, axis=3)
    b_last = b[:, :, :, -1:, :]
    causal = jnp.tril(jnp.ones((C, C), dtype=bool))[:, :, None]
    diff = b[:, :, :, :, None, :] - b[:, :, :, None, :, :]
    decay = jnp.where(causal, jnp.exp(jnp.where(causal, diff, 0.0)), 0.0)
    A = jnp.einsum('bhntd,bhnsd,bhntsd->bhnts', qc, kc, decay)
    o_intra = jnp.einsum('bhnts,bhnsv->bhntv', A, vc)
    U = jnp.einsum('bhnsd,bhnsv->bhndv', kc * jnp.exp(b_last - b), vc)
    chunk_decay = jnp.exp(b_last[:, :, :, 0, :])

    def step(S, inp):
        u, dcy = inp
        return dcy[..., None] * S + u, S

    _, S_prev = lax.scan(step, jnp.zeros((B, H, dk, dv), jnp.float32),
                         (U.transpose(2, 0, 1, 3, 4), chunk_decay.transpose(2, 0, 1, 3)))
    S_prev = S_prev.transpose(1, 2, 0, 3, 4)
    o_inter = jnp.einsum('bhntd,bhndv->bhntv', qc * jnp.exp(b), S_prev)
    o = o_intra + o_inter
    return o.transpose(0, 2, 3, 1, 4).reshape(B, L, H, dv)


def causal_depthwise_conv(x, w, b):
    W = x.shape[-1]
    y = lax.conv_general_dilated(x, w[:, None, :].astype(x.dtype), window_strides=(1,),
                                 padding=[(CONV_W - 1, 0)],
                                 dimension_numbers=('NWC', 'WIO', 'NWC'),
                                 feature_group_count=W)
    return y + b.astype(x.dtype)


def rg_lru(xc, w_r, b_r, w_i, b_i, lam):
    B, L, W = xc.shape
    xb = xc.reshape(B, L, LRU_BLOCKS, LRU_BS)
    r = jax.nn.sigmoid(jnp.einsum('blhi,hij->blhj', xb, w_r).reshape(B, L, W).astype(jnp.float32) + b_r)
    i = jax.nn.sigmoid(jnp.einsum('blhi,hij->blhj', xb, w_i).reshape(B, L, W).astype(jnp.float32) + b_i)
    log_a = -LRU_C * r * jax.nn.softplus(-lam.astype(jnp.float32))
    a = jnp.exp(log_a)
    u = jnp.sqrt(-jnp.expm1(2.0 * log_a)) * (i * xc.astype(jnp.float32))

    def combine(left, right):
        a1, b1 = left
        a2, b2 = right
        return a1 * a2, a2 * b1 + b2

    _, h = lax.associative_scan(combine, (a, u), axis=1)
    return h


def setup_inputs(seed: int = 0) -> dict:
    key = jax.random.key(seed)
    ks = jax.random.split(key, 20)
    f32 = jnp.float32
    nrm = lambda k, shape, s: jax.random.normal(k, shape, f32) * s
    a0 = jax.random.uniform(ks[14], (DEPTH, LRU_WIDTH), f32, 0.9, 0.999)
    a_root = a0 ** (1.0 / LRU_C)
    return {
        "x": nrm(ks[0], (BATCH, SEQ, D_MODEL), 1.0),
        "meta": nrm(ks[1], (N_META, D_MODEL), 1.0),
        "pre_g": 1.0 + nrm(ks[2], (DEPTH, D_MODEL), 0.02),
        "w_in": nrm(ks[3], (DEPTH, D_MODEL, D_IN), D_MODEL ** -0.5),
        "b_f": jax.random.uniform(ks[4], (DEPTH, FOX_HEADS), f32, 1.0, 4.0),
        "w_a2": nrm(ks[5], (DEPTH, GLA_RANK, GLA_KW), GLA_RANK ** -0.5),
        "b_a": nrm(ks[6], (DEPTH, GLA_KW), 0.1),
        "gla_norm_g": 1.0 + nrm(ks[7], (DEPTH, GLA_WIDTH), 0.02),
        "conv_w": nrm(ks[8], (DEPTH, CONV_W, LRU_WIDTH), CONV_W ** -0.5),
        "conv_b": nrm(ks[9], (DEPTH, LRU_WIDTH), 0.02),
        "w_r": nrm(ks[10], (DEPTH, LRU_BLOCKS, LRU_BS, LRU_BS), LRU_BS ** -0.5),
        "b_r": nrm(ks[11], (DEPTH, LRU_WIDTH), 0.02),
        "w_i": nrm(ks[12], (DEPTH, LRU_BLOCKS, LRU_BS, LRU_BS), LRU_BS ** -0.5),
        "b_i": nrm(ks[13], (DEPTH, LRU_WIDTH), 0.02),
        "lru_lambda": jnp.log(a_root) - jnp.log1p(-a_root),
        "w_out": nrm(ks[15], (DEPTH, D_MIX, D_MODEL), D_MIX ** -0.5),
        "post_g": 1.0 + nrm(ks[16], (DEPTH, D_MODEL), 0.02),
    }


def reference(x, meta, pre_g, w_in, b_f, w_a2, b_a, gla_norm_g, conv_w, conv_b,
              w_r, b_r, w_i, b_i, lru_lambda, w_out, post_g):
    B, S, D = x.shape
    dt = x.dtype
    L = S + PAD
    h = jnp.concatenate([jnp.zeros((B, PAD - N_META, D), dt),
                         jnp.broadcast_to(meta.astype(dt)[None], (B, N_META, D)), x], axis=1)
    valid = jnp.arange(L) >= (PAD - N_META)
    vmask = valid[None, :, None].astype(dt)

    for l in range(DEPTH):
        hn = rms_norm(h, pre_g[l])
        proj = hn @ w_in[l]
        (fq, fk, fv, ff, fg, gq, gk, gv, ga, gg, lx, lg) = split_projection(proj)

        log_f = jax.nn.log_sigmoid(ff.astype(jnp.float32) + b_f[l]) * valid[None, :, None]
        o_fox = fox_attention(fq.reshape(B, L, FOX_HEADS, FOX_DH), fk.reshape(B, L, FOX_HEADS, FOX_DH),
                              fv.reshape(B, L, FOX_HEADS, FOX_DH), log_f, valid)
        y_fox = o_fox.astype(dt) * jax.nn.silu(fg)

        log_a = jax.nn.log_sigmoid((ga @ w_a2[l]).astype(jnp.float32) + b_a[l]) / GLA_GATE_NORM
        o_gla = gla_chunked(gq.reshape(B, L, GLA_HEADS, GLA_DK),
                            (gk * vmask).reshape(B, L, GLA_HEADS, GLA_DK),
                            gv.reshape(B, L, GLA_HEADS, GLA_DV),
                            log_a.reshape(B, L, GLA_HEADS, GLA_DK))
        o_gla = o_gla * lax.rsqrt(jnp.mean(o_gla * o_gla, axis=-1, keepdims=True) + EPS)
        o_gla = o_gla.reshape(B, L, GLA_WIDTH) * gla_norm_g[l].astype(jnp.float32)
        y_gla = o_gla.astype(dt) * jax.nn.silu(gg)

        xc = causal_depthwise_conv(lx, conv_w[l], conv_b[l]) * vmask
        h_lru = rg_lru(xc, w_r[l], b_r[l], w_i[l], b_i[l], lru_lambda[l])
        y_lru = h_lru.astype(dt) * jax.nn.silu(lg)

        y = jnp.concatenate([y_fox, y_gla, y_lru], axis=-1) * vmask
        h = h + rms_norm(y @ w_out[l], post_g[l])

    return h[:, PAD:]
```

```python
import functools

import jax
import jax.numpy as jnp
from jax import lax
from jax.experimental import pallas as pl
from jax.experimental.pallas import tpu as pltpu

F32 = jnp.float32
BF16 = jnp.bfloat16

D_MODEL = 1024
N_META = 16
PAD = 128
EPS = 1e-6
BIG = 1e30

FOX_HEADS = 8
FOX_DH = 64
FOX_W = FOX_HEADS * FOX_DH
GLA_HEADS = 4
GLA_DK = 64
GLA_DV = 128
GLA_KW = GLA_HEADS * GLA_DK
GLA_W = GLA_HEADS * GLA_DV
GLA_RANK = 16
GLA_GATE_NORM = 16.0
LRU_W = 1024
LRU_BLOCKS = 8
LRU_BS = LRU_W // LRU_BLOCKS
CONV_W = 4
LRU_C = 8.0
D_MIX = FOX_W + GLA_W + LRU_W

LANES = 128
SUBLANES = 8
VMEM_LIMIT = 56 * 1024 * 1024

QKV_W = 3 * FOX_W
R_FG = 0
R_GQ = R_FG + FOX_W
R_GK = R_GQ + GLA_KW
R_GV = R_GK + GLA_KW
R_GG = R_GV + GLA_W
R_LX = R_GG + GLA_W
R_LG = R_LX + LRU_W
R_SM = R_LG + LRU_W
REST_W = R_SM + LANES

T_SEQ = 128
N_LEVELS = 7
TQ = 128
TK = 128


def _dot(a, b):
    return jnp.dot(a, b, preferred_element_type=F32)


def _dot_nt(a, b):
    return lax.dot_general(a, b, (((1,), (1,)), ((), ())), preferred_element_type=F32)


def _dot_tn(a, b):
    return lax.dot_general(a, b, (((0,), (0,)), ((), ())), preferred_element_type=F32)


def _split3(x):
    hi = x.astype(BF16)
    r = x - hi.astype(F32)
    mid = r.astype(BF16)
    lo = (r - mid.astype(F32)).astype(BF16)
    return hi, mid, lo


def _dot3(m, x):
    hi, mid, lo = _split3(x)
    return _dot(m, hi) + _dot(m, mid) + _dot(m, lo)


def _log_sigmoid(z):
    return jnp.minimum(z, 0.0) - jnp.log1p(jnp.exp(-jnp.abs(z)))


def _sigmoid(z):
    return 1.0 / (1.0 + jnp.exp(-z))


def _silu(z):
    return z * _sigmoid(z)


def _inproj_kernel(h_ref, g_ref, wq_ref, wr_ref, qkv_ref, rest_ref):
    x = h_ref[...]
    ms = jnp.mean(x * x, axis=-1, keepdims=True)
    hn = (x * lax.rsqrt(ms + EPS) * g_ref[...]).astype(BF16)
    step = 512
    for c in range(0, QKV_W, step):
        qkv_ref[:, c:c + step] = _dot(hn, wq_ref[:, c:c + step]).astype(BF16)
    for c in range(0, REST_W, step):
        w = min(step, REST_W - c)
        rest_ref[:, c:c + w] = _dot(hn, wr_ref[:, c:c + w])


def _inproj(h2, g, wq, wr, tm):
    n = h2.shape[0]
    return pl.pallas_call(
        _inproj_kernel,
        grid=(n // tm,),
        in_specs=[
            pl.BlockSpec((tm, D_MODEL), lambda i: (i, 0)),
            pl.BlockSpec((1, D_MODEL), lambda i: (0, 0)),
            pl.BlockSpec((D_MODEL, QKV_W), lambda i: (0, 0)),
            pl.BlockSpec((D_MODEL, REST_W), lambda i: (0, 0)),
        ],
        out_specs=[
            pl.BlockSpec((tm, QKV_W), lambda i: (i, 0)),
            pl.BlockSpec((tm, REST_W), lambda i: (i, 0)),
        ],
        out_shape=[
            jax.ShapeDtypeStruct((n, QKV_W), BF16),
            jax.ShapeDtypeStruct((n, REST_W), F32),
        ],
        compiler_params=pltpu.CompilerParams(
            dimension_semantics=("arbitrary",), vmem_limit_bytes=VMEM_LIMIT),
        name="inproj",
    )(h2, g, wq, wr)


def _seqmix_kernel(gqk_ref, gv_ref, gg_ref, lx_ref, lg_ref, sm_ref,
                   mall_ref, lv_ref, wa2_ref, ba_ref, bf_ref, gng_ref,
                   cw_ref, cb_ref, wri_ref, br_ref, bi_ref, lam_ref,
                   ygla_ref, ylru_ref, ccol_ref, crow_ref,
                   st_ref, cc_ref, px_ref, hc_ref):
    t = pl.program_id(1)
    T = T_SEQ

    @pl.when(t == 0)
    def _():
        st_ref[...] = jnp.zeros_like(st_ref)
        cc_ref[...] = jnp.zeros_like(cc_ref)
        px_ref[...] = jnp.zeros_like(px_ref)
        hc_ref[...] = jnp.zeros_like(hc_ref)

    row = lax.broadcasted_iota(jnp.int32, (T, 1), 0)
    valid = ((t * T + row) >= (PAD - N_META)).astype(F32)
    tri = mall_ref[0:T, :]

    sm = sm_ref[0]
    log_f = _log_sigmoid(sm + bf_ref[...]) * valid
    c = _dot3(tri, log_f) + cc_ref[...]
    cc_ref[...] = c[T - 1:T, :]
    ccol_ref[0] = c
    c_keys = jnp.where(valid > 0.0, c, BIG)
    crow_ref[0] = c_keys.T[0:FOX_HEADS, :]

    z = _dot(sm.astype(BF16), wa2_ref[...]) + ba_ref[...]
    log_a = _log_sigmoid(z) / GLA_GATE_NORM
    cums = _dot3(mall_ref[...], log_a)
    b = cums[0:T]
    gqk = gqk_ref[0]
    q = gqk[:, 0:GLA_KW]
    k = gqk[:, GLA_KW:2 * GLA_KW] * valid
    v = gv_ref[0].astype(BF16)
    lane_k = lax.broadcasted_iota(jnp.int32, (1, GLA_KW), 1)
    head_masks = [(lane_k >= hh * GLA_DK) & (lane_k < (hh + 1) * GLA_DK) for hh in range(GLA_HEADS)]

    def stack_heads(x):
        return jnp.concatenate([jnp.where(mk, x, 0.0).astype(BF16) for mk in head_masks], axis=0)

    lv = lv_ref[...]
    a_st = jnp.where(lv == 0, _dot_nt(stack_heads(q), k.astype(BF16)), 0.0)
    for j in range(1, N_LEVELS + 1):
        bref = cums[j * T:(j + 1) * T]
        qj = q * jnp.exp(jnp.minimum(b - bref, 0.0))
        kj = (k * jnp.exp(jnp.minimum(bref - b, 0.0))).astype(BF16)
        a_st = a_st + jnp.where(lv == j, _dot_nt(stack_heads(qj), kj), 0.0)
    a_st = a_st.astype(BF16)

    b_last = b[T - 1:T, :]
    qd_st = stack_heads(q * jnp.exp(b))
    kd = (k * jnp.exp(b_last - b)).astype(BF16)
    st = st_ref[...]
    st_bf = st.astype(BF16)
    st_new = st * jnp.exp(b_last)
    gg = gg_ref[0]
    for hh in range(GLA_HEADS):
        vh = v[:, hh * GLA_DV:(hh + 1) * GLA_DV]
        o = _dot(a_st[hh * T:(hh + 1) * T], vh) + _dot_nt(qd_st[hh * T:(hh + 1) * T], st_bf)
        o = o * lax.rsqrt(jnp.mean(o * o, axis=-1, keepdims=True) + EPS)
        o = o * gng_ref[:, hh * GLA_DV:(hh + 1) * GLA_DV]
        y = o * _silu(gg[:, hh * GLA_DV:(hh + 1) * GLA_DV]) * valid
        ygla_ref[0, :, hh * GLA_DV:(hh + 1) * GLA_DV] = y.astype(BF16)
        st_new = st_new + jnp.where(head_masks[hh], _dot_tn(vh, kd), 0.0)
    st_ref[...] = st_new

    lx = lx_ref[0]
    prev = px_ref[...]
    row8 = lax.broadcasted_iota(jnp.int32, (SUBLANES, 1), 0)
    xc = lx * cw_ref[CONV_W - 1:CONV_W, :]
    for kk in range(1, CONV_W):
        rolled = pltpu.roll(lx, kk, 0)
        top = jnp.where(row8 < kk, pltpu.roll(prev, kk, 0), rolled[0:SUBLANES])
        shifted = jnp.concatenate([top, rolled[SUBLANES:]], axis=0)
        xc = xc + shifted * cw_ref[CONV_W - 1 - kk:CONV_W - kk, :]
    px_ref[...] = lx[T - SUBLANES:T, :]
    xc = (xc + cb_ref[...]) * valid
    xcb = xc.astype(BF16)
    soft = jnp.maximum(-lam_ref[...], 0.0) + jnp.log1p(jnp.exp(-jnp.abs(lam_ref[...])))
    a_parts, u_parts = [], []
    for blk in range(LRU_BLOCKS):
        sl = slice(blk * LRU_BS, (blk + 1) * LRU_BS)
        ri = _dot(xcb[:, sl], wri_ref[blk])
        r = _sigmoid(ri[:, 0:LRU_BS] + br_ref[:, sl])
        ig = _sigmoid(ri[:, LRU_BS:2 * LRU_BS] + bi_ref[:, sl])
        la = -LRU_C * r * soft[:, sl]
        a_parts.append(jnp.exp(la))
        th = jnp.tanh(la)
        u_parts.append(jnp.sqrt(-2.0 * th / (1.0 - th)) * (ig * xc[:, sl]))
    a = jnp.concatenate(a_parts, axis=1)
    u = jnp.concatenate(u_parts, axis=1)
    d = 1
    while d < T:
        keep = row >= d
        a_sh = jnp.where(keep, pltpu.roll(a, d, 0), 1.0)
        u_sh = jnp.where(keep, pltpu.roll(u, d, 0), 0.0)
        u = a * u_sh + u
        a = a * a_sh
        d *= 2
    hs = a * hc_ref[...] + u
    hc_ref[...] = hs[T - 1:T, :]
    ylru_ref[0] = (hs * _silu(lg_ref[0]) * valid).astype(BF16)


def _seqmix(rest3, consts, B, L):
    T = T_SEQ
    nt = L // T
    const_specs = [pl.BlockSpec(c.shape, functools.partial(lambda nd, b, t: (0,) * nd, c.ndim)) for c in consts]
    return pl.pallas_call(
        _seqmix_kernel,
        grid=(B, nt),
        in_specs=[
            pl.BlockSpec((1, T, 2 * GLA_KW), lambda b, t: (b, t, R_GQ // (2 * GLA_KW))),
            pl.BlockSpec((1, T, GLA_W), lambda b, t: (b, t, R_GV // GLA_W)),
            pl.BlockSpec((1, T, GLA_W), lambda b, t: (b, t, R_GG // GLA_W)),
            pl.BlockSpec((1, T, LRU_W), lambda b, t: (b, t, R_LX // LRU_W)),
            pl.BlockSpec((1, T, LRU_W), lambda b, t: (b, t, R_LG // LRU_W)),
            pl.BlockSpec((1, T, LANES), lambda b, t: (b, t, R_SM // LANES)),
        ] + const_specs,
        out_specs=[
            pl.BlockSpec((1, T, GLA_W), lambda b, t: (b, t, 0)),
            pl.BlockSpec((1, T, LRU_W), lambda b, t: (b, t, 0)),
            pl.BlockSpec((1, T, LANES), lambda b, t: (b, t, 0)),
            pl.BlockSpec((1, FOX_HEADS, T), lambda b, t: (b, 0, t)),
        ],
        out_shape=[
            jax.ShapeDtypeStruct((B, L, GLA_W), BF16),
            jax.ShapeDtypeStruct((B, L, LRU_W), BF16),
            jax.ShapeDtypeStruct((B, L, LANES), F32),
            jax.ShapeDtypeStruct((B, FOX_HEADS, L), F32),
        ],
        scratch_shapes=[
            pltpu.VMEM((GLA_DV, GLA_KW), F32),
            pltpu.VMEM((1, LANES), F32),
            pltpu.VMEM((SUBLANES, LRU_W), F32),
            pltpu.VMEM((1, LRU_W), F32),
        ],
        compiler_params=pltpu.CompilerParams(
            dimension_semantics=("arbitrary", "arbitrary"), vmem_limit_bytes=VMEM_LIMIT),
        name="seqmix",
    )(rest3, rest3, rest3, rest3, rest3, rest3, *consts)


def _fox_kernel(q_ref, k_ref, v_ref, ccol_ref, crow_ref, fg_ref, o_ref):
    hp = pl.program_id(1)
    i = pl.program_id(2)
    lane = lax.broadcasted_iota(jnp.int32, (1, LANES), 1)
    first = lane < FOX_DH
    q = q_ref[0]
    zero = jnp.zeros_like(q)
    qh = (jnp.where(first, q, zero), jnp.where(first, zero, q))
    ccol = ccol_ref[0]
    cq = tuple(jnp.sum(jnp.where(lane == 2 * hp + hh, ccol, 0.0), axis=-1, keepdims=True) for hh in range(2))

    def step(j, carry, diag):
        m, l, acc = carry
        k = k_ref[0, pl.ds(pl.multiple_of(j * TK, TK), TK), :]
        v = v_ref[0, pl.ds(pl.multiple_of(j * TK, TK), TK), :]
        ck = crow_ref[0, 0, :, pl.ds(pl.multiple_of(j * TK, TK), TK)]
        new_m, new_l, alphas, pvs = [], [], [], []
        for hh in range(2):
            s = _dot_nt(qh[hh], k) + cq[hh] - ck[hh:hh + 1, :]
            if diag:
                r_i = lax.broadcasted_iota(jnp.int32, (TQ, TK), 0)
                c_i = lax.broadcasted_iota(jnp.int32, (TQ, TK), 1)
                s = jnp.where(c_i <= r_i, s, -BIG)
            m_new = jnp.maximum(m[hh], jnp.max(s, axis=-1, keepdims=True))
            alpha = jnp.exp(m[hh] - m_new)
            p = jnp.exp(s - m_new)
            new_l.append(alpha * l[hh] + jnp.sum(p, axis=-1, keepdims=True))
            new_m.append(m_new)
            alphas.append(alpha)
            pvs.append(_dot(p.astype(BF16), v))
        acc = acc * jnp.where(first, alphas[0], alphas[1]) + jnp.where(first, pvs[0], pvs[1])
        return tuple(new_m), tuple(new_l), acc

    init_col = jnp.full((TQ, 1), -BIG, F32)
    zero_col = jnp.zeros((TQ, 1), F32)
    carry = ((init_col, init_col), (zero_col, zero_col), jnp.zeros((TQ, LANES), F32))
    carry = lax.fori_loop(0, i, lambda j, c: step(j, c, False), carry)
    m, l, acc = step(i, carry, True)
    o = acc / jnp.where(first, l[0], l[1])
    row = lax.broadcasted_iota(jnp.int32, (TQ, 1), 0)
    valid = ((i * TQ + row) >= (PAD - N_META)).astype(F32)
    o_ref[0] = (o * _silu(fg_ref[0]) * valid).astype(BF16)


def _fox(qkv3, ccol, crow4, rest3, B, L):
    nq = L // TQ
    hpairs = FOX_HEADS // 2
    return pl.pallas_call(
        _fox_kernel,
        grid=(B, hpairs, nq),
        in_specs=[
            pl.BlockSpec((1, TQ, LANES), lambda b, h, i: (b, i, h)),
            pl.BlockSpec((1, L, LANES), lambda b, h, i: (b, 0, FOX_W // LANES + h)),
            pl.BlockSpec((1, L, LANES), lambda b, h, i: (b, 0, 2 * FOX_W // LANES + h)),
            pl.BlockSpec((1, TQ, LANES), lambda b, h, i: (b, i, 0)),
            pl.BlockSpec((1, 1, 2, L), lambda b, h, i: (b, h, 0, 0)),
            pl.BlockSpec((1, TQ, LANES), lambda b, h, i: (b, i, R_FG // LANES + h)),
        ],
        out_specs=pl.BlockSpec((1, TQ, LANES), lambda b, h, i: (b, i, h)),
        out_shape=jax.ShapeDtypeStruct((B, L, FOX_W), BF16),
        compiler_params=pltpu.CompilerParams(
            dimension_semantics=("arbitrary", "arbitrary", "arbitrary"), vmem_limit_bytes=VMEM_LIMIT),
        name="fox",
    )(qkv3, qkv3, qkv3, ccol, crow4, rest3)


def _outproj_kernel(yf_ref, yg_ref, yl_ref, h_ref, w_ref, g_ref, o_ref):
    acc = _dot(yf_ref[...], w_ref[0:FOX_W, :])
    acc = acc + _dot(yg_ref[...], w_ref[FOX_W:FOX_W + GLA_W, :])
    acc = acc + _dot(yl_ref[...], w_ref[FOX_W + GLA_W:D_MIX, :])
    ms = jnp.mean(acc * acc, axis=-1, keepdims=True)
    o_ref[...] = h_ref[...] + acc * lax.rsqrt(ms + EPS) * g_ref[...]


def _outproj(yf, yg, yl, h2, w, g, tm):
    n = h2.shape[0]
    return pl.pallas_call(
        _outproj_kernel,
        grid=(n // tm,),
        in_specs=[
            pl.BlockSpec((tm, FOX_W), lambda i: (i, 0)),
            pl.BlockSpec((tm, GLA_W), lambda i: (i, 0)),
            pl.BlockSpec((tm, LRU_W), lambda i: (i, 0)),
            pl.BlockSpec((tm, D_MODEL), lambda i: (i, 0)),
            pl.BlockSpec((D_MIX, D_MODEL), lambda i: (0, 0)),
            pl.BlockSpec((1, D_MODEL), lambda i: (0, 0)),
        ],
        out_specs=pl.BlockSpec((tm, D_MODEL), lambda i: (i, 0)),
        out_shape=jax.ShapeDtypeStruct((n, D_MODEL), F32),
        compiler_params=pltpu.CompilerParams(
            dimension_semantics=("arbitrary",), vmem_limit_bytes=VMEM_LIMIT),
        name="outproj",
    )(yf, yg, yl, h2, w, g)


def _level_constants():
    T = T_SEQ
    t = jnp.arange(T)[:, None]
    s = jnp.arange(T)[None, :]
    mats = [(s <= t)]
    for j in range(1, N_LEVELS + 1):
        ref_row = (t >> j << j) + (1 << (j - 1)) - 1
        mats.append(s <= ref_row)
    mall = jnp.concatenate(mats, axis=0).astype(BF16)
    x = t ^ s
    level = jnp.zeros((T, T), jnp.int32)
    for j in range(1, N_LEVELS + 1):
        level = jnp.where((x >> (j - 1)) == 1, j, level)
    level = jnp.where(s > t, -1, level)
    return mall, jnp.tile(level, (GLA_HEADS, 1))


def _permute_w_in(w):
    sizes = (FOX_W, FOX_W, FOX_W, FOX_HEADS, FOX_W, GLA_KW, GLA_KW, GLA_W, GLA_RANK, GLA_W, LRU_W, LRU_W)
    offs = [0]
    for sz in sizes:
        offs.append(offs[-1] + sz)
    fq, fk, fv, ff, fg, gq, gk, gv, ga, gg, lx, lg = [w[:, offs[n]:offs[n + 1]] for n in range(len(sizes))]
    wq = jnp.concatenate([fq * (FOX_DH ** -0.5), fk, fv], axis=1).astype(BF16)
    pad = jnp.zeros((w.shape[0], LANES - FOX_HEADS - GLA_RANK), w.dtype)
    wr = jnp.concatenate([fg, gq * (GLA_DK ** -0.5), gk, gv, gg, lx, lg, ff, ga, pad], axis=1).astype(BF16)
    return wq, wr


def kernel(x, meta, pre_g, w_in, b_f, w_a2, b_a, gla_norm_g, conv_w, conv_b, w_r, b_r, w_i, b_i, lru_lambda, w_out, post_g):
    B, S, D = x.shape
    assert D == D_MODEL and S % T_SEQ == 0
    L = S + PAD
    depth = w_in.shape[0]
    dt = x.dtype
    h = jnp.concatenate([jnp.zeros((B, PAD - N_META, D), dt),
                         jnp.broadcast_to(meta.astype(dt)[None], (B, N_META, D)), x], axis=1)
    h2 = h.reshape(B * L, D)
    mall, lv = _level_constants()
    tm_in = 256
    tm_out = 512 if (B * L) % 512 == 0 else 256

    for l in range(depth):
        wq, wr = _permute_w_in(w_in[l])
        qkv, rest = _inproj(h2, pre_g[l][None, :], wq, wr, tm_in)
        rest3 = rest.reshape(B, L, REST_W)

        wa2 = jnp.zeros((LANES, GLA_KW), F32).at[FOX_HEADS:FOX_HEADS + GLA_RANK].set(w_a2[l]).astype(BF16)
        bfp = jnp.zeros((1, LANES), F32).at[0, 0:FOX_HEADS].set(b_f[l])
        wri = jnp.concatenate([w_r[l], w_i[l]], axis=-1).astype(BF16)
        consts = [mall, lv, wa2, b_a[l][None, :], bfp, gla_norm_g[l][None, :],
                  conv_w[l], conv_b[l][None, :], wri, b_r[l][None, :], b_i[l][None, :],
                  lru_lambda[l][None, :]]
        ygla, ylru, ccol, crow = _seqmix(rest3, consts, B, L)

        yfox = _fox(qkv.reshape(B, L, QKV_W), ccol, crow.reshape(B, FOX_HEADS // 2, 2, L), rest3, B, L)

        h2 = _outproj(yfox.reshape(B * L, FOX_W), ygla.reshape(B * L, GLA_W), ylru.reshape(B * L, LRU_W),
                      h2, w_out[l].astype(BF16), post_g[l][None, :], tm_out)

    return h2.reshape(B, L, D)[:, PAD:]
```

```python
import functools

import jax
import jax.numpy as jnp
from jax import lax
from jax.experimental import pallas as pl
from jax.experimental.pallas import tpu as pltpu

F32 = jnp.float32
BF16 = jnp.bfloat16

D_MODEL = 1024
N_META = 16
PAD = 128
EPS = 1e-6
BIG = 1e30

FOX_HEADS = 8
FOX_DH = 64
FOX_W = FOX_HEADS * FOX_DH
GLA_HEADS = 4
GLA_DK = 64
GLA_DV = 128
GLA_KW = GLA_HEADS * GLA_DK
GLA_W = GLA_HEADS * GLA_DV
GLA_RANK = 16
GLA_GATE_NORM = 16.0
LRU_W = 1024
LRU_BLOCKS = 8
LRU_BS = LRU_W // LRU_BLOCKS
CONV_W = 4
LRU_C = 8.0
D_MIX = FOX_W + GLA_W + LRU_W

LANES = 128
SUBLANES = 8
VMEM_LIMIT = 56 * 1024 * 1024

QKV_W = 3 * FOX_W
R_FG = 0
R_GQ = R_FG + FOX_W
R_GK = R_GQ + GLA_KW
R_GV = R_GK + GLA_KW
R_GG = R_GV + GLA_W
R_LX = R_GG + GLA_W
R_LG = R_LX + LRU_W
R_SM = R_LG + LRU_W
REST_W = R_SM + LANES

T_SEQ = 128
N_LEVELS = 7
TQ = 640
TK = 640
LOG2E = 1.4426950408889634


def _dot(a, b):
    return jnp.dot(a, b, preferred_element_type=F32)


def _dot_nt(a, b):
    return lax.dot_general(a, b, (((1,), (1,)), ((), ())), preferred_element_type=F32)


def _dot_tn(a, b):
    return lax.dot_general(a, b, (((0,), (0,)), ((), ())), preferred_element_type=F32)


def _split3(x):
    hi = x.astype(BF16)
    r = x - hi.astype(F32)
    mid = r.astype(BF16)
    lo = (r - mid.astype(F32)).astype(BF16)
    return hi, mid, lo


def _dot3(m, x):
    hi, mid, lo = _split3(x)
    return _dot(m, hi) + _dot(m, mid) + _dot(m, lo)


def _log_sigmoid(z):
    return jnp.minimum(z, 0.0) - jnp.log1p(jnp.exp(-jnp.abs(z)))


def _sigmoid(z):
    return 1.0 / (1.0 + jnp.exp(-z))


def _silu(z):
    return z * _sigmoid(z)


def _inproj_kernel(h_ref, g_ref, wq_ref, wr_ref, qkv_ref, rest_ref):
    x = h_ref[...]
    ms = jnp.mean(x * x, axis=-1, keepdims=True)
    hn = (x * lax.rsqrt(ms + EPS) * g_ref[...]).astype(BF16)
    step = 512
    for c in range(0, QKV_W, step):
        qkv_ref[:, c:c + step] = _dot(hn, wq_ref[:, c:c + step]).astype(BF16)
    for c in range(0, REST_W, step):
        w = min(step, REST_W - c)
        rest_ref[:, c:c + w] = _dot(hn, wr_ref[:, c:c + w])


def _inproj(h2, g, wq, wr, tm):
    n = h2.shape[0]
    return pl.pallas_call(
        _inproj_kernel,
        grid=(n // tm,),
        in_specs=[
            pl.BlockSpec((tm, D_MODEL), lambda i: (i, 0)),
            pl.BlockSpec((1, D_MODEL), lambda i: (0, 0)),
            pl.BlockSpec((D_MODEL, QKV_W), lambda i: (0, 0)),
            pl.BlockSpec((D_MODEL, REST_W), lambda i: (0, 0)),
        ],
        out_specs=[
            pl.BlockSpec((tm, QKV_W), lambda i: (i, 0)),
            pl.BlockSpec((tm, REST_W), lambda i: (i, 0)),
        ],
        out_shape=[
            jax.ShapeDtypeStruct((n, QKV_W), BF16),
            jax.ShapeDtypeStruct((n, REST_W), F32),
        ],
        compiler_params=pltpu.CompilerParams(
            dimension_semantics=("arbitrary",), vmem_limit_bytes=VMEM_LIMIT),
        name="inproj",
    )(h2, g, wq, wr)


def _seqmix_kernel(gqk_ref, gv_ref, gg_ref, lx_ref, lg_ref, sm_ref,
                   mall_ref, lv_ref, wa2_ref, ba_ref, bf_ref, gng_ref,
                   cw_ref, cb_ref, wri_ref, br_ref, bi_ref, lam_ref,
                   ygla_ref, ylru_ref, ccol_ref, crow_ref,
                   st_ref, cc_ref, px_ref, hc_ref):
    t = pl.program_id(1)
    T = T_SEQ

    @pl.when(t == 0)
    def _():
        st_ref[...] = jnp.zeros_like(st_ref)
        cc_ref[...] = jnp.zeros_like(cc_ref)
        px_ref[...] = jnp.zeros_like(px_ref)
        hc_ref[...] = jnp.zeros_like(hc_ref)

    row = lax.broadcasted_iota(jnp.int32, (T, 1), 0)
    valid = ((t * T + row) >= (PAD - N_META)).astype(F32)
    tri = mall_ref[0:T, :]

    sm = sm_ref[0]
    log_f = _log_sigmoid(sm + bf_ref[...]) * valid
    c = _dot3(tri, log_f) + cc_ref[...]
    cc_ref[...] = c[T - 1:T, :]
    c2 = c * LOG2E
    ccol_ref[0] = c2
    c_keys = jnp.where(valid > 0.0, c2, BIG)
    crow_ref[0] = c_keys.T[0:FOX_HEADS, :]

    z = _dot(sm.astype(BF16), wa2_ref[...]) + ba_ref[...]
    log_a = _log_sigmoid(z) / GLA_GATE_NORM
    cums = _dot3(mall_ref[...], log_a)
    b = cums[0:T]
    gqk = gqk_ref[0]
    q = gqk[:, 0:GLA_KW]
    k = gqk[:, GLA_KW:2 * GLA_KW] * valid
    v = gv_ref[0].astype(BF16)
    lane_k = lax.broadcasted_iota(jnp.int32, (1, GLA_KW), 1)
    head_masks = [(lane_k >= hh * GLA_DK) & (lane_k < (hh + 1) * GLA_DK) for hh in range(GLA_HEADS)]

    def stack_heads(x):
        return jnp.concatenate([jnp.where(mk, x, 0.0).astype(BF16) for mk in head_masks], axis=0)

    lv = lv_ref[...]
    a_st = jnp.where(lv == 0, _dot_nt(stack_heads(q), k.astype(BF16)), 0.0)
    for j in range(1, N_LEVELS + 1):
        bref = cums[j * T:(j + 1) * T]
        qj = q * jnp.exp(jnp.minimum(b - bref, 0.0))
        kj = (k * jnp.exp(jnp.minimum(bref - b, 0.0))).astype(BF16)
        a_st = a_st + jnp.where(lv == j, _dot_nt(stack_heads(qj), kj), 0.0)
    a_st = a_st.astype(BF16)

    b_last = b[T - 1:T, :]
    qd_st = stack_heads(q * jnp.exp(b))
    kd = (k * jnp.exp(b_last - b)).astype(BF16)
    st = st_ref[...]
    st_bf = st.astype(BF16)
    st_new = st * jnp.exp(b_last)
    gg = gg_ref[0]
    for hh in range(GLA_HEADS):
        vh = v[:, hh * GLA_DV:(hh + 1) * GLA_DV]
        o = _dot(a_st[hh * T:(hh + 1) * T], vh) + _dot_nt(qd_st[hh * T:(hh + 1) * T], st_bf)
        o = o * lax.rsqrt(jnp.mean(o * o, axis=-1, keepdims=True) + EPS)
        o = o * gng_ref[:, hh * GLA_DV:(hh + 1) * GLA_DV]
        y = o * _silu(gg[:, hh * GLA_DV:(hh + 1) * GLA_DV]) * valid
        ygla_ref[0, :, hh * GLA_DV:(hh + 1) * GLA_DV] = y.astype(BF16)
        st_new = st_new + jnp.where(head_masks[hh], _dot_tn(vh, kd), 0.0)
    st_ref[...] = st_new

    lx = lx_ref[0]
    prev = px_ref[...]
    row8 = lax.broadcasted_iota(jnp.int32, (SUBLANES, 1), 0)
    xc = lx * cw_ref[CONV_W - 1:CONV_W, :]
    for kk in range(1, CONV_W):
        rolled = pltpu.roll(lx, kk, 0)
        top = jnp.where(row8 < kk, pltpu.roll(prev, kk, 0), rolled[0:SUBLANES])
        shifted = jnp.concatenate([top, rolled[SUBLANES:]], axis=0)
        xc = xc + shifted * cw_ref[CONV_W - 1 - kk:CONV_W - kk, :]
    px_ref[...] = lx[T - SUBLANES:T, :]
    xc = (xc + cb_ref[...]) * valid
    xcb = xc.astype(BF16)
    soft = jnp.maximum(-lam_ref[...], 0.0) + jnp.log1p(jnp.exp(-jnp.abs(lam_ref[...])))
    a_parts, u_parts = [], []
    for blk in range(LRU_BLOCKS):
        sl = slice(blk * LRU_BS, (blk + 1) * LRU_BS)
        ri = _dot(xcb[:, sl], wri_ref[blk])
        r = _sigmoid(ri[:, 0:LRU_BS] + br_ref[:, sl])
        ig = _sigmoid(ri[:, LRU_BS:2 * LRU_BS] + bi_ref[:, sl])
        la = -LRU_C * r * soft[:, sl]
        a_parts.append(jnp.exp(la))
        th = jnp.tanh(la)
        u_parts.append(jnp.sqrt(-2.0 * th / (1.0 - th)) * (ig * xc[:, sl]))
    a = jnp.concatenate(a_parts, axis=1)
    u = jnp.concatenate(u_parts, axis=1)
    d = 1
    while d < T:
        keep = row >= d
        a_sh = jnp.where(keep, pltpu.roll(a, d, 0), 1.0)
        u_sh = jnp.where(keep, pltpu.roll(u, d, 0), 0.0)
        u = a * u_sh + u
        a = a * a_sh
        d *= 2
    hs = a * hc_ref[...] + u
    hc_ref[...] = hs[T - 1:T, :]
    ylru_ref[0] = (hs * _silu(lg_ref[0]) * valid).astype(BF16)


def _seqmix(rest3, consts, B, L):
    T = T_SEQ
    nt = L // T
    const_specs = [pl.BlockSpec(c.shape, functools.partial(lambda nd, b, t: (0,) * nd, c.ndim)) for c in consts]
    return pl.pallas_call(
        _seqmix_kernel,
        grid=(B, nt),
        in_specs=[
            pl.BlockSpec((1, T, 2 * GLA_KW), lambda b, t: (b, t, R_GQ // (2 * GLA_KW))),
            pl.BlockSpec((1, T, GLA_W), lambda b, t: (b, t, R_GV // GLA_W)),
            pl.BlockSpec((1, T, GLA_W), lambda b, t: (b, t, R_GG // GLA_W)),
            pl.BlockSpec((1, T, LRU_W), lambda b, t: (b, t, R_LX // LRU_W)),
            pl.BlockSpec((1, T, LRU_W), lambda b, t: (b, t, R_LG // LRU_W)),
            pl.BlockSpec((1, T, LANES), lambda b, t: (b, t, R_SM // LANES)),
        ] + const_specs,
        out_specs=[
            pl.BlockSpec((1, T, GLA_W), lambda b, t: (b, t, 0)),
            pl.BlockSpec((1, T, LRU_W), lambda b, t: (b, t, 0)),
            pl.BlockSpec((1, T, LANES), lambda b, t: (b, t, 0)),
            pl.BlockSpec((1, FOX_HEADS, T), lambda b, t: (b, 0, t)),
        ],
        out_shape=[
            jax.ShapeDtypeStruct((B, L, GLA_W), BF16),
            jax.ShapeDtypeStruct((B, L, LRU_W), BF16),
            jax.ShapeDtypeStruct((B, L, LANES), F32),
            jax.ShapeDtypeStruct((B, FOX_HEADS, L), F32),
        ],
        scratch_shapes=[
            pltpu.VMEM((GLA_DV, GLA_KW), F32),
            pltpu.VMEM((1, LANES), F32),
            pltpu.VMEM((SUBLANES, LRU_W), F32),
            pltpu.VMEM((1, LRU_W), F32),
        ],
        compiler_params=pltpu.CompilerParams(
            dimension_semantics=("arbitrary", "arbitrary"), vmem_limit_bytes=VMEM_LIMIT),
        name="seqmix",
    )(rest3, rest3, rest3, rest3, rest3, rest3, *consts)


def _fox_kernel(q_ref, k_ref, v_ref, ccol_ref, crow_ref, fg_ref, o_ref):
    hp = pl.program_id(1)
    i = pl.program_id(2)
    lane = lax.broadcasted_iota(jnp.int32, (1, LANES), 1)
    first = lane < FOX_DH
    q = q_ref[0]
    zero = jnp.zeros_like(q)
    qh = (jnp.where(first, q, zero), jnp.where(first, zero, q))
    ccol = ccol_ref[0]
    cq = tuple(jnp.sum(jnp.where(lane == 2 * hp + hh, ccol, 0.0), axis=-1, keepdims=True) for hh in range(2))
    lane2 = lax.broadcasted_iota(jnp.int32, (1, 2 * LANES), 1)
    sel0 = (lane2 < FOX_DH) | (lane2 == LANES)
    ones_blk = jnp.broadcast_to((lane < 2).astype(BF16), (TK, LANES))

    def step(j, carry, diag):
        m, acc = carry
        off = pl.multiple_of(j * TK, LANES)
        k = k_ref[0, pl.ds(off, TK), :]
        vext = jnp.concatenate([v_ref[0, pl.ds(off, TK), :], ones_blk], axis=1)
        ck = crow_ref[0, 0, :, pl.ds(off, TK)]
        new_m, alphas, pvs = [], [], []
        for hh in range(2):
            s = _dot_nt(qh[hh], k) + cq[hh] - ck[hh:hh + 1, :]
            if diag:
                r_i = lax.broadcasted_iota(jnp.int32, (TQ, TK), 0)
                c_i = lax.broadcasted_iota(jnp.int32, (TQ, TK), 1)
                s = jnp.where(c_i <= r_i, s, -BIG)
            m_new = jnp.maximum(m[hh], jnp.max(s, axis=-1, keepdims=True))
            alphas.append(jnp.exp2(m[hh] - m_new))
            new_m.append(m_new)
            pvs.append(_dot(jnp.exp2(s - m_new).astype(BF16), vext))
        acc = acc * jnp.where(sel0, alphas[0], alphas[1]) + jnp.where(sel0, pvs[0], pvs[1])
        return tuple(new_m), acc

    init_col = jnp.full((TQ, 1), -BIG, F32)
    carry = ((init_col, init_col), jnp.zeros((TQ, 2 * LANES), F32))
    carry = lax.fori_loop(0, i, lambda j, c: step(j, c, False), carry)
    _, acc = step(i, carry, True)
    o = acc[:, 0:LANES] / jnp.where(first, acc[:, LANES:LANES + 1], acc[:, LANES + 1:LANES + 2])
    row = lax.broadcasted_iota(jnp.int32, (TQ, 1), 0)
    valid = ((i * TQ + row) >= (PAD - N_META)).astype(F32)
    o_ref[0] = (o * _silu(fg_ref[0]) * valid).astype(BF16)


def _fox(qkv3, ccol, crow4, rest3, B, L):
    nq = L // TQ
    hpairs = FOX_HEADS // 2
    return pl.pallas_call(
        _fox_kernel,
        grid=(B, hpairs, nq),
        in_specs=[
            pl.BlockSpec((1, TQ, LANES), lambda b, h, i: (b, i, h)),
            pl.BlockSpec((1, L, LANES), lambda b, h, i: (b, 0, FOX_W // LANES + h)),
            pl.BlockSpec((1, L, LANES), lambda b, h, i: (b, 0, 2 * FOX_W // LANES + h)),
            pl.BlockSpec((1, TQ, LANES), lambda b, h, i: (b, i, 0)),
            pl.BlockSpec((1, 1, 2, L), lambda b, h, i: (b, h, 0, 0)),
            pl.BlockSpec((1, TQ, LANES), lambda b, h, i: (b, i, R_FG // LANES + h)),
        ],
        out_specs=pl.BlockSpec((1, TQ, LANES), lambda b, h, i: (b, i, h)),
        out_shape=jax.ShapeDtypeStruct((B, L, FOX_W), BF16),
        compiler_params=pltpu.CompilerParams(
            dimension_semantics=("arbitrary", "arbitrary", "arbitrary"), vmem_limit_bytes=VMEM_LIMIT),
        name="fox",
    )(qkv3, qkv3, qkv3, ccol, crow4, rest3)


def _outproj_kernel(yf_ref, yg_ref, yl_ref, h_ref, w_ref, g_ref, o_ref):
    acc = _dot(yf_ref[...], w_ref[0:FOX_W, :])
    acc = acc + _dot(yg_ref[...], w_ref[FOX_W:FOX_W + GLA_W, :])
    acc = acc + _dot(yl_ref[...], w_ref[FOX_W + GLA_W:D_MIX, :])
    ms = jnp.mean(acc * acc, axis=-1, keepdims=True)
    o_ref[...] = h_ref[...] + acc * lax.rsqrt(ms + EPS) * g_ref[...]


def _outproj(yf, yg, yl, h2, w, g, tm):
    n = h2.shape[0]
    return pl.pallas_call(
        _outproj_kernel,
        grid=(n // tm,),
        in_specs=[
            pl.BlockSpec((tm, FOX_W), lambda i: (i, 0)),
            pl.BlockSpec((tm, GLA_W), lambda i: (i, 0)),
            pl.BlockSpec((tm, LRU_W), lambda i: (i, 0)),
            pl.BlockSpec((tm, D_MODEL), lambda i: (i, 0)),
            pl.BlockSpec((D_MIX, D_MODEL), lambda i: (0, 0)),
            pl.BlockSpec((1, D_MODEL), lambda i: (0, 0)),
        ],
        out_specs=pl.BlockSpec((tm, D_MODEL), lambda i: (i, 0)),
        out_shape=jax.ShapeDtypeStruct((n, D_MODEL), F32),
        compiler_params=pltpu.CompilerParams(
            dimension_semantics=("arbitrary",), vmem_limit_bytes=VMEM_LIMIT),
        name="outproj",
    )(yf, yg, yl, h2, w, g)


def _level_constants():
    T = T_SEQ
    t = jnp.arange(T)[:, None]
    s = jnp.arange(T)[None, :]
    mats = [(s <= t)]
    for j in range(1, N_LEVELS + 1):
        ref_row = (t >> j << j) + (1 << (j - 1)) - 1
        mats.append(s <= ref_row)
    mall = jnp.concatenate(mats, axis=0).astype(BF16)
    x = t ^ s
    level = jnp.zeros((T, T), jnp.int32)
    for j in range(1, N_LEVELS + 1):
        level = jnp.where((x >> (j - 1)) == 1, j, level)
    level = jnp.where(s > t, -1, level)
    return mall, jnp.tile(level, (GLA_HEADS, 1))


def _permute_w_in(w):
    sizes = (FOX_W, FOX_W, FOX_W, FOX_HEADS, FOX_W, GLA_KW, GLA_KW, GLA_W, GLA_RANK, GLA_W, LRU_W, LRU_W)
    offs = [0]
    for sz in sizes:
        offs.append(offs[-1] + sz)
    fq, fk, fv, ff, fg, gq, gk, gv, ga, gg, lx, lg = [w[:, offs[n]:offs[n + 1]] for n in range(len(sizes))]
    wq = jnp.concatenate([fq * (FOX_DH ** -0.5 * LOG2E), fk, fv], axis=1).astype(BF16)
    pad = jnp.zeros((w.shape[0], LANES - FOX_HEADS - GLA_RANK), w.dtype)
    wr = jnp.concatenate([fg, gq * (GLA_DK ** -0.5), gk, gv, gg, lx, lg, ff, ga, pad], axis=1).astype(BF16)
    return wq, wr


def kernel(x, meta, pre_g, w_in, b_f, w_a2, b_a, gla_norm_g, conv_w, conv_b, w_r, b_r, w_i, b_i, lru_lambda, w_out, post_g):
    B, S, D = x.shape
    L = S + PAD
    assert D == D_MODEL and L % T_SEQ == 0 and L % TQ == 0
    depth = w_in.shape[0]
    dt = x.dtype
    h = jnp.concatenate([jnp.zeros((B, PAD - N_META, D), dt),
                         jnp.broadcast_to(meta.astype(dt)[None], (B, N_META, D)), x], axis=1)
    h2 = h.reshape(B * L, D)
    mall, lv = _level_constants()
    tm_in = 256
    tm_out = 512 if (B * L) % 512 == 0 else 256

    for l in range(depth):
        wq, wr = _permute_w_in(w_in[l])
        qkv, rest = _inproj(h2, pre_g[l][None, :], wq, wr, tm_in)
        rest3 = rest.reshape(B, L, REST_W)

        wa2 = jnp.zeros((LANES, GLA_KW), F32).at[FOX_HEADS:FOX_HEADS + GLA_RANK].set(w_a2[l]).astype(BF16)
        bfp = jnp.zeros((1, LANES), F32).at[0, 0:FOX_HEADS].set(b_f[l])
        wri = jnp.concatenate([w_r[l], w_i[l]], axis=-1).astype(BF16)
        consts = [mall, lv, wa2, b_a[l][None, :], bfp, gla_norm_g[l][None, :],
                  conv_w[l], conv_b[l][None, :], wri, b_r[l][None, :], b_i[l][None, :],
                  lru_lambda[l][None, :]]
        ygla, ylru, ccol, crow = _seqmix(rest3, consts, B, L)

        yfox = _fox(qkv.reshape(B, L, QKV_W), ccol, crow.reshape(B, FOX_HEADS // 2, 2, L), rest3, B, L)

        h2 = _outproj(yfox.reshape(B * L, FOX_W), ygla.reshape(B * L, GLA_W), ylru.reshape(B * L, LRU_W),
                      h2, w_out[l].astype(BF16), post_g[l][None, :], tm_out)

    return h2.reshape(B, L, D)[:, PAD:]
```

```python
import functools

import jax
import jax.numpy as jnp
from jax import lax
from jax.experimental import pallas as pl
from jax.experimental.pallas import tpu as pltpu

F32 = jnp.float32
BF16 = jnp.bfloat16

D_MODEL = 1024
N_META = 16
PAD = 128
EPS = 1e-6
BIG = 1e30

FOX_HEADS = 8
FOX_DH = 64
FOX_W = FOX_HEADS * FOX_DH
GLA_HEADS = 4
GLA_DK = 64
GLA_DV = 128
GLA_KW = GLA_HEADS * GLA_DK
GLA_W = GLA_HEADS * GLA_DV
GLA_RANK = 16
GLA_GATE_NORM = 16.0
LRU_W = 1024
LRU_BLOCKS = 8
LRU_BS = LRU_W // LRU_BLOCKS
CONV_W = 4
LRU_C = 8.0
D_MIX = FOX_W + GLA_W + LRU_W

LANES = 128
SUBLANES = 8
VMEM_LIMIT = 56 * 1024 * 1024

QKV_W = 3 * FOX_W
R_FG = 0
R_GQ = R_FG + FOX_W
R_GK = R_GQ + GLA_KW
R_GV = R_GK + GLA_KW
R_GG = R_GV + GLA_W
R_LX = R_GG + GLA_W
R_LG = R_LX + LRU_W
R_SM = R_LG + LRU_W
REST_W = R_SM + LANES

T_SEQ = 128
N_LEVELS = 7
TQ = 640
TK = 640
LOG2E = 1.4426950408889634
SKIP_LOG2 = 160.0


def _dot(a, b):
    return jnp.dot(a, b, preferred_element_type=F32)


def _dot_nt(a, b):
    return lax.dot_general(a, b, (((1,), (1,)), ((), ())), preferred_element_type=F32)


def _dot_tn(a, b):
    return lax.dot_general(a, b, (((0,), (0,)), ((), ())), preferred_element_type=F32)


def _split3(x):
    hi = x.astype(BF16)
    r = x - hi.astype(F32)
    mid = r.astype(BF16)
    lo = (r - mid.astype(F32)).astype(BF16)
    return hi, mid, lo


def _dot3(m, x):
    hi, mid, lo = _split3(x)
    return _dot(m, hi) + _dot(m, mid) + _dot(m, lo)


def _log_sigmoid(z):
    return jnp.minimum(z, 0.0) - jnp.log(1.0 + jnp.exp(-jnp.abs(z)))


def _sigmoid(z):
    return 1.0 / (1.0 + jnp.exp(-z))


def _silu(z):
    return z * _sigmoid(z)


def _inproj_kernel(h_ref, g_ref, wq_ref, wr_ref, qkv_ref, rest_ref):
    x = h_ref[...]
    ms = jnp.mean(x * x, axis=-1, keepdims=True)
    hn = (x * lax.rsqrt(ms + EPS) * g_ref[...]).astype(BF16)
    step = 512
    for c in range(0, QKV_W, step):
        qkv_ref[:, c:c + step] = _dot(hn, wq_ref[:, c:c + step]).astype(BF16)
    for c in range(0, REST_W, step):
        w = min(step, REST_W - c)
        rest_ref[:, c:c + w] = _dot(hn, wr_ref[:, c:c + w])


def _inproj(h2, g, wq, wr, tm):
    n = h2.shape[0]
    return pl.pallas_call(
        _inproj_kernel,
        grid=(n // tm,),
        in_specs=[
            pl.BlockSpec((tm, D_MODEL), lambda i: (i, 0)),
            pl.BlockSpec((1, D_MODEL), lambda i: (0, 0)),
            pl.BlockSpec((D_MODEL, QKV_W), lambda i: (0, 0)),
            pl.BlockSpec((D_MODEL, REST_W), lambda i: (0, 0)),
        ],
        out_specs=[
            pl.BlockSpec((tm, QKV_W), lambda i: (i, 0)),
            pl.BlockSpec((tm, REST_W), lambda i: (i, 0)),
        ],
        out_shape=[
            jax.ShapeDtypeStruct((n, QKV_W), BF16),
            jax.ShapeDtypeStruct((n, REST_W), F32),
        ],
        compiler_params=pltpu.CompilerParams(
            dimension_semantics=("arbitrary",), vmem_limit_bytes=VMEM_LIMIT),
        name="inproj",
    )(h2, g, wq, wr)


def _seqmix_kernel(gqk_ref, gv_ref, gg_ref, lx_ref, lg_ref, sm_ref,
                   mall_ref, lv_ref, wa2_ref, ba_ref, bf_ref, gng_ref,
                   cw_ref, cb_ref, wri_ref, br_ref, bi_ref, lam_ref,
                   ygla_ref, ylru_ref, ccol_ref, crow_ref,
                   st_ref, cc_ref, px_ref, hc_ref):
    t = pl.program_id(1)
    T = T_SEQ

    @pl.when(t == 0)
    def _():
        st_ref[...] = jnp.zeros_like(st_ref)
        cc_ref[...] = jnp.zeros_like(cc_ref)
        px_ref[...] = jnp.zeros_like(px_ref)
        hc_ref[...] = jnp.zeros_like(hc_ref)

    row = lax.broadcasted_iota(jnp.int32, (T, 1), 0)
    valid = ((t * T + row) >= (PAD - N_META)).astype(F32)
    tri = mall_ref[0:T, :]

    sm = sm_ref[0]
    log_f = _log_sigmoid(sm + bf_ref[...]) * valid
    c = _dot3(tri, log_f) + cc_ref[...]
    cc_ref[...] = c[T - 1:T, :]
    c2 = c * LOG2E
    ccol_ref[0] = c2
    c_keys = jnp.where(valid > 0.0, c2, BIG)
    crow_ref[0] = c_keys.T[0:FOX_HEADS, :]

    z = _dot(sm.astype(BF16), wa2_ref[...]) + ba_ref[...]
    log_a = _log_sigmoid(z) / GLA_GATE_NORM
    cums = _dot3(mall_ref[...], log_a)
    b = cums[0:T]
    gqk = gqk_ref[0]
    q = gqk[:, 0:GLA_KW]
    k = gqk[:, GLA_KW:2 * GLA_KW] * valid
    v = gv_ref[0].astype(BF16)
    lane_k = lax.broadcasted_iota(jnp.int32, (1, GLA_KW), 1)
    head_masks = [(lane_k >= hh * GLA_DK) & (lane_k < (hh + 1) * GLA_DK) for hh in range(GLA_HEADS)]

    def stack_heads(x):
        return jnp.concatenate([jnp.where(mk, x, 0.0).astype(BF16) for mk in head_masks], axis=0)

    lv = lv_ref[...]
    a_st = jnp.where(lv == 0, _dot_nt(stack_heads(q), k.astype(BF16)), 0.0)
    for j in range(1, N_LEVELS + 1):
        bref = cums[j * T:(j + 1) * T]
        qj = q * jnp.exp(jnp.minimum(b - bref, 0.0))
        kj = (k * jnp.exp(jnp.minimum(bref - b, 0.0))).astype(BF16)
        a_st = a_st + jnp.where(lv == j, _dot_nt(stack_heads(qj), kj), 0.0)
    a_st = a_st.astype(BF16)

    b_last = b[T - 1:T, :]
    qd_st = stack_heads(q * jnp.exp(b))
    kd = (k * jnp.exp(b_last - b)).astype(BF16)
    st = st_ref[...]
    st_bf = st.astype(BF16)
    st_new = st * jnp.exp(b_last)
    gg = gg_ref[0]
    for hh in range(GLA_HEADS):
        vh = v[:, hh * GLA_DV:(hh + 1) * GLA_DV]
        o = _dot(a_st[hh * T:(hh + 1) * T], vh) + _dot_nt(qd_st[hh * T:(hh + 1) * T], st_bf)
        o = o * lax.rsqrt(jnp.mean(o * o, axis=-1, keepdims=True) + EPS)
        o = o * gng_ref[:, hh * GLA_DV:(hh + 1) * GLA_DV]
        y = o * _silu(gg[:, hh * GLA_DV:(hh + 1) * GLA_DV]) * valid
        ygla_ref[0, :, hh * GLA_DV:(hh + 1) * GLA_DV] = y.astype(BF16)
        st_new = st_new + jnp.where(head_masks[hh], _dot_tn(vh, kd), 0.0)
    st_ref[...] = st_new

    groups = T // SUBLANES
    sub = lax.broadcasted_iota(jnp.int32, (SUBLANES, 1), 0)
    lxg = [px_ref[...]] + [lx_ref[0, g * SUBLANES:(g + 1) * SUBLANES, :] for g in range(groups)]
    px_ref[...] = lxg[groups]
    rolled = {}
    xc_parts = []
    for g in range(1, groups + 1):
        acc = lxg[g] * cw_ref[CONV_W - 1:CONV_W, :]
        for kk in range(1, CONV_W):
            for gi in (g - 1, g):
                if (gi, kk) not in rolled:
                    rolled[(gi, kk)] = pltpu.roll(lxg[gi], kk, 0)
            shifted = jnp.where(sub >= kk, rolled[(g, kk)], rolled[(g - 1, kk)])
            acc = acc + shifted * cw_ref[CONV_W - 1 - kk:CONV_W - kk, :]
        xc_parts.append(acc)
    xc = (jnp.concatenate(xc_parts, axis=0) + cb_ref[...]) * valid
    xcb = xc.astype(BF16)
    soft = jnp.maximum(-lam_ref[...], 0.0) + jnp.log(1.0 + jnp.exp(-jnp.abs(lam_ref[...])))
    a_parts, u_parts = [], []
    for blk in range(LRU_BLOCKS):
        sl = slice(blk * LRU_BS, (blk + 1) * LRU_BS)
        ri = _dot(xcb[:, sl], wri_ref[blk])
        r = _sigmoid(ri[:, 0:LRU_BS] + br_ref[:, sl])
        ig = _sigmoid(ri[:, LRU_BS:2 * LRU_BS] + bi_ref[:, sl])
        la = -LRU_C * r * soft[:, sl]
        a_parts.append(jnp.exp(la))
        th = jnp.tanh(la)
        u_parts.append(jnp.sqrt(-2.0 * th / (1.0 - th)) * (ig * xc[:, sl]))
    a = jnp.concatenate(a_parts, axis=1)
    u = jnp.concatenate(u_parts, axis=1)
    h_prev = hc_ref[...]
    hs_parts = []
    for g in range(groups):
        ag = a[g * SUBLANES:(g + 1) * SUBLANES, :]
        ug = u[g * SUBLANES:(g + 1) * SUBLANES, :]
        d = 1
        while d < SUBLANES:
            keep = sub >= d
            a_sh = jnp.where(keep, pltpu.roll(ag, d, 0), 1.0)
            u_sh = jnp.where(keep, pltpu.roll(ug, d, 0), 0.0)
            ug = ag * u_sh + ug
            ag = ag * a_sh
            d *= 2
        hg = ag * h_prev + ug
        h_prev = hg[SUBLANES - 1:SUBLANES, :]
        hs_parts.append(hg)
    hc_ref[...] = h_prev
    hs = jnp.concatenate(hs_parts, axis=0)
    ylru_ref[0] = (hs * _silu(lg_ref[0]) * valid).astype(BF16)


def _seqmix(rest3, consts, B, L):
    T = T_SEQ
    nt = L // T
    const_specs = [pl.BlockSpec(c.shape, functools.partial(lambda nd, b, t: (0,) * nd, c.ndim)) for c in consts]
    return pl.pallas_call(
        _seqmix_kernel,
        grid=(B, nt),
        in_specs=[
            pl.BlockSpec((1, T, 2 * GLA_KW), lambda b, t: (b, t, R_GQ // (2 * GLA_KW))),
            pl.BlockSpec((1, T, GLA_W), lambda b, t: (b, t, R_GV // GLA_W)),
            pl.BlockSpec((1, T, GLA_W), lambda b, t: (b, t, R_GG // GLA_W)),
            pl.BlockSpec((1, T, LRU_W), lambda b, t: (b, t, R_LX // LRU_W)),
            pl.BlockSpec((1, T, LRU_W), lambda b, t: (b, t, R_LG // LRU_W)),
            pl.BlockSpec((1, T, LANES), lambda b, t: (b, t, R_SM // LANES)),
        ] + const_specs,
        out_specs=[
            pl.BlockSpec((1, T, GLA_W), lambda b, t: (b, t, 0)),
            pl.BlockSpec((1, T, LRU_W), lambda b, t: (b, t, 0)),
            pl.BlockSpec((1, T, LANES), lambda b, t: (b, t, 0)),
            pl.BlockSpec((1, FOX_HEADS, T), lambda b, t: (b, 0, t)),
        ],
        out_shape=[
            jax.ShapeDtypeStruct((B, L, GLA_W), BF16),
            jax.ShapeDtypeStruct((B, L, LRU_W), BF16),
            jax.ShapeDtypeStruct((B, L, LANES), F32),
            jax.ShapeDtypeStruct((B, FOX_HEADS, L), F32),
        ],
        scratch_shapes=[
            pltpu.VMEM((GLA_DV, GLA_KW), F32),
            pltpu.VMEM((1, LANES), F32),
            pltpu.VMEM((SUBLANES, LRU_W), F32),
            pltpu.VMEM((1, LRU_W), F32),
        ],
        compiler_params=pltpu.CompilerParams(
            dimension_semantics=("arbitrary", "arbitrary"), vmem_limit_bytes=VMEM_LIMIT),
        name="seqmix",
    )(rest3, rest3, rest3, rest3, rest3, rest3, *consts)


def _head_sqnorm_max(x, first):
    x2 = x.astype(F32)
    x2 = x2 * x2
    n0 = jnp.sum(jnp.where(first, x2, 0.0), axis=-1, keepdims=True)
    n1 = jnp.sum(jnp.where(first, 0.0, x2), axis=-1, keepdims=True)
    return jnp.max(n0, axis=0, keepdims=True), jnp.max(n1, axis=0, keepdims=True)


def _fox_kernel(q_ref, k_ref, v_ref, ccol_ref, crow_ref, fg_ref, o_ref, kst_ref):
    hp = pl.program_id(1)
    i = pl.program_id(2)
    nkb = k_ref.shape[1] // TK
    lane = lax.broadcasted_iota(jnp.int32, (1, LANES), 1)
    first = lane < FOX_DH

    @pl.when(i == 0)
    def _():
        kst_ref[...] = jnp.zeros_like(kst_ref)
        for j in range(nkb):
            n0, n1 = _head_sqnorm_max(k_ref[0, j * TK:(j + 1) * TK, :], first)
            cmin = jnp.min(crow_ref[0, 0, :, j * TK:(j + 1) * TK], axis=-1, keepdims=True)
            kst_ref[j:j + 1, :] = jnp.where(lane == 0, n0, jnp.where(lane == 1, n1, jnp.where(
                lane == 2, cmin[0:1], jnp.where(lane == 3, cmin[1:2], 0.0))))

    q = q_ref[0]
    zero = jnp.zeros_like(q)
    qh = (jnp.where(first, q, zero), jnp.where(first, zero, q))
    ccol = ccol_ref[0]
    cq = tuple(jnp.sum(jnp.where(lane == 2 * hp + hh, ccol, 0.0), axis=-1, keepdims=True) for hh in range(2))

    qn = _head_sqnorm_max(q, first)
    kst = kst_ref[...]
    rowj = lax.broadcasted_iota(jnp.int32, (kst.shape[0], 1), 0)
    starts = []
    for hh in range(2):
        kn = kst[:, hh:hh + 1] * 1.01
        cmin = kst[:, 2 + hh:3 + hh]
        kn_i = jnp.sum(jnp.where(rowj == i, kn, 0.0), axis=0, keepdims=True)
        ub = jnp.max(cq[hh], axis=0, keepdims=True) - cmin + jnp.sqrt(qn[hh] * kn) + jnp.sqrt(qn[hh] * kn_i)
        skippable = (ub <= -SKIP_LOG2) & (rowj < i)
        starts.append(jnp.min(jnp.where(skippable, kst.shape[0], rowj)))
    j0 = jnp.minimum(starts[0], starts[1])
    lane2 = lax.broadcasted_iota(jnp.int32, (1, 2 * LANES), 1)
    sel0 = (lane2 < FOX_DH) | (lane2 == LANES)
    ones_blk = jnp.broadcast_to((lane < 2).astype(BF16), (TK, LANES))

    def step(j, carry, diag):
        m, acc = carry
        off = pl.multiple_of(j * TK, LANES)
        k = k_ref[0, pl.ds(off, TK), :]
        vext = jnp.concatenate([v_ref[0, pl.ds(off, TK), :], ones_blk], axis=1)
        ck = crow_ref[0, 0, :, pl.ds(off, TK)]
        new_m, alphas, pvs = [], [], []
        for hh in range(2):
            s = _dot_nt(qh[hh], k) + cq[hh] - ck[hh:hh + 1, :]
            if diag:
                r_i = lax.broadcasted_iota(jnp.int32, (TQ, TK), 0)
                c_i = lax.broadcasted_iota(jnp.int32, (TQ, TK), 1)
                s = jnp.where(c_i <= r_i, s, -BIG)
            m_new = jnp.maximum(m[hh], jnp.max(s, axis=-1, keepdims=True))
            alphas.append(jnp.exp2(m[hh] - m_new))
            new_m.append(m_new)
            pvs.append(_dot(jnp.exp2(s - m_new).astype(BF16), vext))
        acc = acc * jnp.where(sel0, alphas[0], alphas[1]) + jnp.where(sel0, pvs[0], pvs[1])
        return tuple(new_m), acc

    init_col = jnp.full((TQ, 1), -BIG, F32)
    carry = ((init_col, init_col), jnp.zeros((TQ, 2 * LANES), F32))
    carry = lax.fori_loop(j0, i, lambda j, c: step(j, c, False), carry)
    _, acc = step(i, carry, True)
    o = acc[:, 0:LANES] / jnp.where(first, acc[:, LANES:LANES + 1], acc[:, LANES + 1:LANES + 2])
    row = lax.broadcasted_iota(jnp.int32, (TQ, 1), 0)
    valid = ((i * TQ + row) >= (PAD - N_META)).astype(F32)
    o_ref[0] = (o * _silu(fg_ref[0]) * valid).astype(BF16)


def _fox(qkv3, ccol, crow4, rest3, B, L):
    nq = L // TQ
    hpairs = FOX_HEADS // 2
    return pl.pallas_call(
        _fox_kernel,
        grid=(B, hpairs, nq),
        in_specs=[
            pl.BlockSpec((1, TQ, LANES), lambda b, h, i: (b, i, h)),
            pl.BlockSpec((1, L, LANES), lambda b, h, i: (b, 0, FOX_W // LANES + h)),
            pl.BlockSpec((1, L, LANES), lambda b, h, i: (b, 0, 2 * FOX_W // LANES + h)),
            pl.BlockSpec((1, TQ, LANES), lambda b, h, i: (b, i, 0)),
            pl.BlockSpec((1, 1, 2, L), lambda b, h, i: (b, h, 0, 0)),
            pl.BlockSpec((1, TQ, LANES), lambda b, h, i: (b, i, R_FG // LANES + h)),
        ],
        out_specs=pl.BlockSpec((1, TQ, LANES), lambda b, h, i: (b, i, h)),
        out_shape=jax.ShapeDtypeStruct((B, L, FOX_W), BF16),
        scratch_shapes=[pltpu.VMEM((2 * SUBLANES, LANES), F32)],
        compiler_params=pltpu.CompilerParams(
            dimension_semantics=("arbitrary", "arbitrary", "arbitrary"), vmem_limit_bytes=VMEM_LIMIT),
        name="fox",
    )(qkv3, qkv3, qkv3, ccol, crow4, rest3)


def _outproj_kernel(yf_ref, yg_ref, yl_ref, h_ref, w_ref, g_ref, o_ref):
    acc = _dot(yf_ref[...], w_ref[0:FOX_W, :])
    acc = acc + _dot(yg_ref[...], w_ref[FOX_W:FOX_W + GLA_W, :])
    acc = acc + _dot(yl_ref[...], w_ref[FOX_W + GLA_W:D_MIX, :])
    ms = jnp.mean(acc * acc, axis=-1, keepdims=True)
    o_ref[...] = h_ref[...] + acc * lax.rsqrt(ms + EPS) * g_ref[...]


def _outproj(yf, yg, yl, h2, w, g, tm):
    n = h2.shape[0]
    return pl.pallas_call(
        _outproj_kernel,
        grid=(n // tm,),
        in_specs=[
            pl.BlockSpec((tm, FOX_W), lambda i: (i, 0)),
            pl.BlockSpec((tm, GLA_W), lambda i: (i, 0)),
            pl.BlockSpec((tm, LRU_W), lambda i: (i, 0)),
            pl.BlockSpec((tm, D_MODEL), lambda i: (i, 0)),
            pl.BlockSpec((D_MIX, D_MODEL), lambda i: (0, 0)),
            pl.BlockSpec((1, D_MODEL), lambda i: (0, 0)),
        ],
        out_specs=pl.BlockSpec((tm, D_MODEL), lambda i: (i, 0)),
        out_shape=jax.ShapeDtypeStruct((n, D_MODEL), F32),
        compiler_params=pltpu.CompilerParams(
            dimension_semantics=("arbitrary",), vmem_limit_bytes=VMEM_LIMIT),
        name="outproj",
    )(yf, yg, yl, h2, w, g)


def _level_constants():
    T = T_SEQ
    t = jnp.arange(T)[:, None]
    s = jnp.arange(T)[None, :]
    mats = [(s <= t)]
    for j in range(1, N_LEVELS + 1):
        ref_row = (t >> j << j) + (1 << (j - 1)) - 1
        mats.append(s <= ref_row)
    mall = jnp.concatenate(mats, axis=0).astype(BF16)
    x = t ^ s
    level = jnp.zeros((T, T), jnp.int32)
    for j in range(1, N_LEVELS + 1):
        level = jnp.where((x >> (j - 1)) == 1, j, level)
    level = jnp.where(s > t, -1, level)
    return mall, jnp.tile(level, (GLA_HEADS, 1))


def _permute_w_in(w):
    sizes = (FOX_W, FOX_W, FOX_W, FOX_HEADS, FOX_W, GLA_KW, GLA_KW, GLA_W, GLA_RANK, GLA_W, LRU_W, LRU_W)
    offs = [0]
    for sz in sizes:
        offs.append(offs[-1] + sz)
    fq, fk, fv, ff, fg, gq, gk, gv, ga, gg, lx, lg = [w[:, offs[n]:offs[n + 1]] for n in range(len(sizes))]
    wq = jnp.concatenate([fq * (FOX_DH ** -0.5 * LOG2E), fk, fv], axis=1).astype(BF16)
    pad = jnp.zeros((w.shape[0], LANES - FOX_HEADS - GLA_RANK), w.dtype)
    wr = jnp.concatenate([fg, gq * (GLA_DK ** -0.5), gk, gv, gg, lx, lg, ff, ga, pad], axis=1).astype(BF16)
    return wq, wr


def kernel(x, meta, pre_g, w_in, b_f, w_a2, b_a, gla_norm_g, conv_w, conv_b, w_r, b_r, w_i, b_i, lru_lambda, w_out, post_g):
    B, S, D = x.shape
    L = S + PAD
    assert D == D_MODEL and L % T_SEQ == 0 and L % TQ == 0
    depth = w_in.shape[0]
    dt = x.dtype
    h = jnp.concatenate([jnp.zeros((B, PAD - N_META, D), dt),
                         jnp.broadcast_to(meta.astype(dt)[None], (B, N_META, D)), x], axis=1)
    h2 = h.reshape(B * L, D)
    mall, lv = _level_constants()
    tm_in = 256
    tm_out = 512 if (B * L) % 512 == 0 else 256

    for l in range(depth):
        wq, wr = _permute_w_in(w_in[l])
        qkv, rest = _inproj(h2, pre_g[l][None, :], wq, wr, tm_in)
        rest3 = rest.reshape(B, L, REST_W)

        wa2 = jnp.zeros((LANES, GLA_KW), F32).at[FOX_HEADS:FOX_HEADS + GLA_RANK].set(w_a2[l]).astype(BF16)
        bfp = jnp.zeros((1, LANES), F32).at[0, 0:FOX_HEADS].set(b_f[l])
        wri = jnp.concatenate([w_r[l], w_i[l]], axis=-1).astype(BF16)
        consts = [mall, lv, wa2, b_a[l][None, :], bfp, gla_norm_g[l][None, :],
                  conv_w[l], conv_b[l][None, :], wri, b_r[l][None, :], b_i[l][None, :],
                  lru_lambda[l][None, :]]
        ygla, ylru, ccol, crow = _seqmix(rest3, consts, B, L)

        yfox = _fox(qkv.reshape(B, L, QKV_W), ccol, crow.reshape(B, FOX_HEADS // 2, 2, L), rest3, B, L)

        h2 = _outproj(yfox.reshape(B * L, FOX_W), ygla.reshape(B * L, GLA_W), ylru.reshape(B * L, LRU_W),
                      h2, w_out[l].astype(BF16), post_g[l][None, :], tm_out)

    return h2.reshape(B, L, D)[:, PAD:]
```

```python
import functools

import jax
import jax.numpy as jnp
from jax import lax
from jax.experimental import pallas as pl
from jax.experimental.pallas import tpu as pltpu

F32 = jnp.float32
BF16 = jnp.bfloat16

D_MODEL = 1024
N_META = 16
PAD = 128
EPS = 1e-6
BIG = 1e30

FOX_HEADS = 8
FOX_DH = 64
FOX_W = FOX_HEADS * FOX_DH
GLA_HEADS = 4
GLA_DK = 64
GLA_DV = 128
GLA_KW = GLA_HEADS * GLA_DK
GLA_W = GLA_HEADS * GLA_DV
GLA_RANK = 16
GLA_GATE_NORM = 16.0
LRU_W = 1024
LRU_BLOCKS = 8
LRU_BS = LRU_W // LRU_BLOCKS
CONV_W = 4
LRU_C = 8.0
D_MIX = FOX_W + GLA_W + LRU_W

LANES = 128
SUBLANES = 8
VMEM_LIMIT = 56 * 1024 * 1024

QKV_W = 3 * FOX_W
R_FG = 0
R_GQ = R_FG + FOX_W
R_GK = R_GQ + GLA_KW
R_GV = R_GK + GLA_KW
R_GG = R_GV + GLA_W
R_LX = R_GG + GLA_W
R_LG = R_LX + LRU_W
R_SM = R_LG + LRU_W
REST_W = R_SM + LANES

T_SEQ = 128
N_LEVELS = 7
TQ = 640
TK = 640
LOG2E = 1.4426950408889634
SKIP_LOG2 = 160.0


def _dot(a, b):
    return jnp.dot(a, b, preferred_element_type=F32)


def _dot_nt(a, b):
    return lax.dot_general(a, b, (((1,), (1,)), ((), ())), preferred_element_type=F32)


def _dot_tn(a, b):
    return lax.dot_general(a, b, (((0,), (0,)), ((), ())), preferred_element_type=F32)


def _split3(x):
    hi = x.astype(BF16)
    r = x - hi.astype(F32)
    mid = r.astype(BF16)
    lo = (r - mid.astype(F32)).astype(BF16)
    return hi, mid, lo


def _dot3(m, x):
    hi, mid, lo = _split3(x)
    return _dot(m, hi) + _dot(m, mid) + _dot(m, lo)


def _log_sigmoid(z):
    return jnp.minimum(z, 0.0) - jnp.log(1.0 + jnp.exp2(jnp.abs(z) * -LOG2E))


def _sigmoid(z):
    return 1.0 / (1.0 + jnp.exp2(z * -LOG2E))


def _silu(z):
    return z * _sigmoid(z)


def _inproj_kernel(h_ref, g_ref, wq_ref, wr_ref, qkv_ref, rest_ref):
    x = h_ref[...]
    ms = jnp.mean(x * x, axis=-1, keepdims=True)
    hn = (x * lax.rsqrt(ms + EPS) * g_ref[...]).astype(BF16)
    step = 512
    for c in range(0, QKV_W, step):
        qkv_ref[:, c:c + step] = _dot(hn, wq_ref[:, c:c + step]).astype(BF16)
    for c in range(0, REST_W, step):
        w = min(step, REST_W - c)
        rest_ref[:, c:c + w] = _dot(hn, wr_ref[:, c:c + w])


def _inproj(h2, g, wq, wr, tm):
    n = h2.shape[0]
    return pl.pallas_call(
        _inproj_kernel,
        grid=(n // tm,),
        in_specs=[
            pl.BlockSpec((tm, D_MODEL), lambda i: (i, 0)),
            pl.BlockSpec((1, D_MODEL), lambda i: (0, 0)),
            pl.BlockSpec((D_MODEL, QKV_W), lambda i: (0, 0)),
            pl.BlockSpec((D_MODEL, REST_W), lambda i: (0, 0)),
        ],
        out_specs=[
            pl.BlockSpec((tm, QKV_W), lambda i: (i, 0)),
            pl.BlockSpec((tm, REST_W), lambda i: (i, 0)),
        ],
        out_shape=[
            jax.ShapeDtypeStruct((n, QKV_W), BF16),
            jax.ShapeDtypeStruct((n, REST_W), F32),
        ],
        compiler_params=pltpu.CompilerParams(
            dimension_semantics=("arbitrary",), vmem_limit_bytes=VMEM_LIMIT),
        name="inproj",
    )(h2, g, wq, wr)


def _level_reference(b, j, sub):
    T = b.shape[0]
    half = 1 << (j - 1)
    parts = []
    if j > 3:
        for blk in range(T >> j):
            r = (blk << j) + half - 1
            parts.append(jnp.broadcast_to(b[r:r + 1, :], (1 << j, b.shape[1])))
        return jnp.concatenate(parts, axis=0)
    for g in range(T // SUBLANES):
        x = b[g * SUBLANES:(g + 1) * SUBLANES, :]
        if j == 3:
            parts.append(jnp.broadcast_to(x[half - 1:half, :], x.shape))
            continue
        ref = x
        for off in range(-(half - 1), half + 1):
            if off != 0:
                pick = (sub & ((1 << j) - 1)) == (half - 1 + off)
                ref = jnp.where(pick, pltpu.roll(x, off % SUBLANES, 0), ref)
        parts.append(ref)
    return jnp.concatenate(parts, axis=0)


def _seqmix_kernel(gqk_ref, gv_ref, gg_ref, lx_ref, lg_ref, sm_ref,
                   mall_ref, lv_ref, wa2_ref, ba_ref, bf_ref, gng_ref,
                   cw_ref, cb_ref, wri_ref, br_ref, bi_ref, lam_ref,
                   ygla_ref, ylru_ref, ccol_ref, crow_ref,
                   st_ref, cc_ref, px_ref, hc_ref):
    t = pl.program_id(1)
    T = T_SEQ

    @pl.when(t == 0)
    def _():
        st_ref[...] = jnp.zeros_like(st_ref)
        cc_ref[...] = jnp.zeros_like(cc_ref)
        px_ref[...] = jnp.zeros_like(px_ref)
        hc_ref[...] = jnp.zeros_like(hc_ref)

    row = lax.broadcasted_iota(jnp.int32, (T, 1), 0)
    valid = ((t * T + row) >= (PAD - N_META)).astype(F32)
    tri = mall_ref[...]

    groups = T // SUBLANES
    sub = lax.broadcasted_iota(jnp.int32, (SUBLANES, 1), 0)

    def lru_stages():
        lxg = [px_ref[...]] + [lx_ref[0, g * SUBLANES:(g + 1) * SUBLANES, :] for g in range(groups)]
        px_ref[...] = lxg[groups]
        rolled = {}
        xc_parts = []
        for g in range(1, groups + 1):
            acc = lxg[g] * cw_ref[CONV_W - 1:CONV_W, :]
            for kk in range(1, CONV_W):
                for gi in (g - 1, g):
                    if (gi, kk) not in rolled:
                        rolled[(gi, kk)] = pltpu.roll(lxg[gi], kk, 0)
                shifted = jnp.where(sub >= kk, rolled[(g, kk)], rolled[(g - 1, kk)])
                acc = acc + shifted * cw_ref[CONV_W - 1 - kk:CONV_W - kk, :]
            xc_parts.append(acc)
            if g % 4 == 0:
                yield
        xc = (jnp.concatenate(xc_parts, axis=0) + cb_ref[...]) * valid
        xcb = xc.astype(BF16)
        soft = jnp.maximum(-lam_ref[...], 0.0) + jnp.log(1.0 + jnp.exp(-jnp.abs(lam_ref[...])))
        a_parts, u_parts = [], []
        for blk in range(LRU_BLOCKS):
            sl = slice(blk * LRU_BS, (blk + 1) * LRU_BS)
            ri = _dot(xcb[:, sl], wri_ref[blk])
            r = _sigmoid(ri[:, 0:LRU_BS] + br_ref[:, sl])
            ig = _sigmoid(ri[:, LRU_BS:2 * LRU_BS] + bi_ref[:, sl])
            la = -LRU_C * r * soft[:, sl]
            a_parts.append(jnp.exp(la))
            th = jnp.tanh(la)
            w = -2.0 * th / (1.0 - th)
            root = jnp.where(w > 0.0, w * lax.rsqrt(w), 0.0)
            u_parts.append(root * (ig * xc[:, sl]))
            yield
        a = jnp.concatenate(a_parts, axis=1)
        u = jnp.concatenate(u_parts, axis=1)
        h_prev = hc_ref[...]
        lg = lg_ref[0]
        for g in range(groups):
            rows = slice(g * SUBLANES, (g + 1) * SUBLANES)
            ag, ug = a[rows, :], u[rows, :]
            d = 1
            while d < SUBLANES:
                keep = sub >= d
                a_sh = jnp.where(keep, pltpu.roll(ag, d, 0), 1.0)
                u_sh = jnp.where(keep, pltpu.roll(ug, d, 0), 0.0)
                ug = ag * u_sh + ug
                ag = ag * a_sh
                d *= 2
            hg = ag * h_prev + ug
            h_prev = hg[SUBLANES - 1:SUBLANES, :]
            ylru_ref[0, rows, :] = (hg * _silu(lg[rows, :]) * valid[rows, :]).astype(BF16)
            if g % 4 == 3:
                yield
        hc_ref[...] = h_prev

    def gla_stages():
        sm = sm_ref[0]
        log_f = _log_sigmoid(sm + bf_ref[...]) * valid
        z = _dot(sm.astype(BF16), wa2_ref[...]) + ba_ref[...]
        log_a = _log_sigmoid(z) / GLA_GATE_NORM
        yield
        c = _dot3(tri, log_f) + cc_ref[...]
        cc_ref[...] = c[T - 1:T, :]
        c2 = c * LOG2E
        ccol_ref[0] = c2
        c_keys = jnp.where(valid > 0.0, c2, BIG)
        crow_ref[0] = c_keys.T[0:FOX_HEADS, :]
        b = _dot3(tri, log_a)
        yield
        gqk = gqk_ref[0]
        q = gqk[:, 0:GLA_KW]
        k = gqk[:, GLA_KW:2 * GLA_KW] * valid
        v = gv_ref[0].astype(BF16)
        lane_k = lax.broadcasted_iota(jnp.int32, (1, GLA_KW), 1)
        head_masks = [(lane_k >= hh * GLA_DK) & (lane_k < (hh + 1) * GLA_DK) for hh in range(GLA_HEADS)]

        def stack_heads(x):
            return jnp.concatenate([jnp.where(mk, x, 0.0).astype(BF16) for mk in head_masks], axis=0)

        lv = lv_ref[...]
        a_st = jnp.where(lv == 0, _dot_nt(stack_heads(q), k.astype(BF16)), 0.0)
        yield
        for j in range(1, N_LEVELS + 1):
            bref = _level_reference(b, j, sub)
            e = jnp.exp2(jnp.abs(b - bref) * -LOG2E)
            a_st = a_st + jnp.where(lv == j, _dot_nt(stack_heads(q * e), (k * e).astype(BF16)), 0.0)
            yield
        a_st = a_st.astype(BF16)
        b_last = b[T - 1:T, :]
        qd_st = stack_heads(q * jnp.exp(b))
        kd = (k * jnp.exp(b_last - b)).astype(BF16)
        st = st_ref[...]
        st_bf = st.astype(BF16)
        st_new = st * jnp.exp(b_last)
        gg = gg_ref[0]
        for hh in range(GLA_HEADS):
            vh = v[:, hh * GLA_DV:(hh + 1) * GLA_DV]
            o = _dot(a_st[hh * T:(hh + 1) * T], vh) + _dot_nt(qd_st[hh * T:(hh + 1) * T], st_bf)
            o = o * lax.rsqrt(jnp.mean(o * o, axis=-1, keepdims=True) + EPS)
            o = o * gng_ref[:, hh * GLA_DV:(hh + 1) * GLA_DV]
            y = o * _silu(gg[:, hh * GLA_DV:(hh + 1) * GLA_DV]) * valid
            ygla_ref[0, :, hh * GLA_DV:(hh + 1) * GLA_DV] = y.astype(BF16)
            st_new = st_new + jnp.where(head_masks[hh], _dot_tn(vh, kd), 0.0)
            yield
        st_ref[...] = st_new

    streams = [gla_stages(), lru_stages()]
    while streams:
        for gen in list(streams):
            if next(gen, StopIteration) is StopIteration:
                streams.remove(gen)


def _seqmix(rest3, consts, B, L):
    T = T_SEQ
    nt = L // T
    const_specs = [pl.BlockSpec(c.shape, functools.partial(lambda nd, b, t: (0,) * nd, c.ndim)) for c in consts]
    return pl.pallas_call(
        _seqmix_kernel,
        grid=(B, nt),
        in_specs=[
            pl.BlockSpec((1, T, 2 * GLA_KW), lambda b, t: (b, t, R_GQ // (2 * GLA_KW))),
            pl.BlockSpec((1, T, GLA_W), lambda b, t: (b, t, R_GV // GLA_W)),
            pl.BlockSpec((1, T, GLA_W), lambda b, t: (b, t, R_GG // GLA_W)),
            pl.BlockSpec((1, T, LRU_W), lambda b, t: (b, t, R_LX // LRU_W)),
            pl.BlockSpec((1, T, LRU_W), lambda b, t: (b, t, R_LG // LRU_W)),
            pl.BlockSpec((1, T, LANES), lambda b, t: (b, t, R_SM // LANES)),
        ] + const_specs,
        out_specs=[
            pl.BlockSpec((1, T, GLA_W), lambda b, t: (b, t, 0)),
            pl.BlockSpec((1, T, LRU_W), lambda b, t: (b, t, 0)),
            pl.BlockSpec((1, T, LANES), lambda b, t: (b, t, 0)),
            pl.BlockSpec((1, FOX_HEADS, T), lambda b, t: (b, 0, t)),
        ],
        out_shape=[
            jax.ShapeDtypeStruct((B, L, GLA_W), BF16),
            jax.ShapeDtypeStruct((B, L, LRU_W), BF16),
            jax.ShapeDtypeStruct((B, L, LANES), F32),
            jax.ShapeDtypeStruct((B, FOX_HEADS, L), F32),
        ],
        scratch_shapes=[
            pltpu.VMEM((GLA_DV, GLA_KW), F32),
            pltpu.VMEM((1, LANES), F32),
            pltpu.VMEM((SUBLANES, LRU_W), F32),
            pltpu.VMEM((1, LRU_W), F32),
        ],
        compiler_params=pltpu.CompilerParams(
            dimension_semantics=("arbitrary", "arbitrary"), vmem_limit_bytes=VMEM_LIMIT),
        name="seqmix",
    )(rest3, rest3, rest3, rest3, rest3, rest3, *consts)


def _head_sqnorm_max(x, first):
    x2 = x.astype(F32)
    x2 = x2 * x2
    n0 = jnp.sum(jnp.where(first, x2, 0.0), axis=-1, keepdims=True)
    n1 = jnp.sum(jnp.where(first, 0.0, x2), axis=-1, keepdims=True)
    return jnp.max(n0, axis=0, keepdims=True), jnp.max(n1, axis=0, keepdims=True)


def _fox_kernel(q_ref, k_ref, v_ref, ccol_ref, crow_ref, fg_ref, o_ref, kst_ref):
    hp = pl.program_id(1)
    i = pl.program_id(2)
    nkb = k_ref.shape[1] // TK
    lane = lax.broadcasted_iota(jnp.int32, (1, LANES), 1)
    first = lane < FOX_DH

    @pl.when(i == 0)
    def _():
        kst_ref[...] = jnp.zeros_like(kst_ref)
        for j in range(nkb):
            n0, n1 = _head_sqnorm_max(k_ref[0, j * TK:(j + 1) * TK, :], first)
            cmin = jnp.min(crow_ref[0, 0, :, j * TK:(j + 1) * TK], axis=-1, keepdims=True)
            kst_ref[j:j + 1, :] = jnp.where(lane == 0, n0, jnp.where(lane == 1, n1, jnp.where(
                lane == 2, cmin[0:1], jnp.where(lane == 3, cmin[1:2], 0.0))))

    q = q_ref[0]
    zero = jnp.zeros_like(q)
    qh = (jnp.where(first, q, zero), jnp.where(first, zero, q))
    ccol = ccol_ref[0]
    cq = tuple(jnp.sum(jnp.where(lane == 2 * hp + hh, ccol, 0.0), axis=-1, keepdims=True) for hh in range(2))

    qn = _head_sqnorm_max(q, first)
    kst = kst_ref[...]
    rowj = lax.broadcasted_iota(jnp.int32, (kst.shape[0], 1), 0)
    starts = []
    for hh in range(2):
        kn = kst[:, hh:hh + 1] * 1.01
        cmin = kst[:, 2 + hh:3 + hh]
        kn_i = jnp.sum(jnp.where(rowj == i, kn, 0.0), axis=0, keepdims=True)
        ub = jnp.max(cq[hh], axis=0, keepdims=True) - cmin + jnp.sqrt(qn[hh] * kn) + jnp.sqrt(qn[hh] * kn_i)
        skippable = (ub <= -SKIP_LOG2) & (rowj < i)
        starts.append(jnp.min(jnp.where(skippable, kst.shape[0], rowj)))
    j0 = jnp.minimum(starts[0], starts[1])
    lane2 = lax.broadcasted_iota(jnp.int32, (1, 2 * LANES), 1)
    sel0 = (lane2 < FOX_DH) | (lane2 == LANES)
    ones_blk = jnp.broadcast_to((lane < 2).astype(BF16), (TK, LANES))

    def step(j, carry, diag):
        m, acc = carry
        off = pl.multiple_of(j * TK, LANES)
        k = k_ref[0, pl.ds(off, TK), :]
        vext = jnp.concatenate([v_ref[0, pl.ds(off, TK), :], ones_blk], axis=1)
        ck = crow_ref[0, 0, :, pl.ds(off, TK)]
        new_m, alphas, pvs = [], [], []
        for hh in range(2):
            s = _dot_nt(qh[hh], k) + cq[hh] - ck[hh:hh + 1, :]
            if diag:
                r_i = lax.broadcasted_iota(jnp.int32, (TQ, TK), 0)
                c_i = lax.broadcasted_iota(jnp.int32, (TQ, TK), 1)
                s = jnp.where(c_i <= r_i, s, -BIG)
            m_new = jnp.maximum(m[hh], jnp.max(s, axis=-1, keepdims=True))
            alphas.append(jnp.exp2(m[hh] - m_new))
            new_m.append(m_new)
            pvs.append(_dot(jnp.exp2(s - m_new).astype(BF16), vext))
        acc = acc * jnp.where(sel0, alphas[0], alphas[1]) + jnp.where(sel0, pvs[0], pvs[1])
        return tuple(new_m), acc

    init_col = jnp.full((TQ, 1), -BIG, F32)
    carry = ((init_col, init_col), jnp.zeros((TQ, 2 * LANES), F32))
    carry = lax.fori_loop(j0, i, lambda j, c: step(j, c, False), carry)
    _, acc = step(i, carry, True)
    o = acc[:, 0:LANES] / jnp.where(first, acc[:, LANES:LANES + 1], acc[:, LANES + 1:LANES + 2])
    row = lax.broadcasted_iota(jnp.int32, (TQ, 1), 0)
    valid = ((i * TQ + row) >= (PAD - N_META)).astype(F32)
    o_ref[0] = (o * _silu(fg_ref[0]) * valid).astype(BF16)


def _fox(qkv3, ccol, crow4, rest3, B, L):
    nq = L // TQ
    hpairs = FOX_HEADS // 2
    return pl.pallas_call(
        _fox_kernel,
        grid=(B, hpairs, nq),
        in_specs=[
            pl.BlockSpec((1, TQ, LANES), lambda b, h, i: (b, i, h)),
            pl.BlockSpec((1, L, LANES), lambda b, h, i: (b, 0, FOX_W // LANES + h)),
            pl.BlockSpec((1, L, LANES), lambda b, h, i: (b, 0, 2 * FOX_W // LANES + h)),
            pl.BlockSpec((1, TQ, LANES), lambda b, h, i: (b, i, 0)),
            pl.BlockSpec((1, 1, 2, L), lambda b, h, i: (b, h, 0, 0)),
            pl.BlockSpec((1, TQ, LANES), lambda b, h, i: (b, i, R_FG // LANES + h)),
        ],
        out_specs=pl.BlockSpec((1, TQ, LANES), lambda b, h, i: (b, i, h)),
        out_shape=jax.ShapeDtypeStruct((B, L, FOX_W), BF16),
        scratch_shapes=[pltpu.VMEM((2 * SUBLANES, LANES), F32)],
        compiler_params=pltpu.CompilerParams(
            dimension_semantics=("arbitrary", "arbitrary", "arbitrary"), vmem_limit_bytes=VMEM_LIMIT),
        name="fox",
    )(qkv3, qkv3, qkv3, ccol, crow4, rest3)


def _outproj_kernel(yf_ref, yg_ref, yl_ref, h_ref, w_ref, g_ref, o_ref):
    acc = _dot(yf_ref[...], w_ref[0:FOX_W, :])
    acc = acc + _dot(yg_ref[...], w_ref[FOX_W:FOX_W + GLA_W, :])
    acc = acc + _dot(yl_ref[...], w_ref[FOX_W + GLA_W:D_MIX, :])
    ms = jnp.mean(acc * acc, axis=-1, keepdims=True)
    o_ref[...] = h_ref[...] + acc * lax.rsqrt(ms + EPS) * g_ref[...]


def _outproj(yf, yg, yl, h2, w, g, tm):
    n = h2.shape[0]
    return pl.pallas_call(
        _outproj_kernel,
        grid=(n // tm,),
        in_specs=[
            pl.BlockSpec((tm, FOX_W), lambda i: (i, 0)),
            pl.BlockSpec((tm, GLA_W), lambda i: (i, 0)),
            pl.BlockSpec((tm, LRU_W), lambda i: (i, 0)),
            pl.BlockSpec((tm, D_MODEL), lambda i: (i, 0)),
            pl.BlockSpec((D_MIX, D_MODEL), lambda i: (0, 0)),
            pl.BlockSpec((1, D_MODEL), lambda i: (0, 0)),
        ],
        out_specs=pl.BlockSpec((tm, D_MODEL), lambda i: (i, 0)),
        out_shape=jax.ShapeDtypeStruct((n, D_MODEL), F32),
        compiler_params=pltpu.CompilerParams(
            dimension_semantics=("arbitrary",), vmem_limit_bytes=VMEM_LIMIT),
        name="outproj",
    )(yf, yg, yl, h2, w, g)


def _level_constants():
    T = T_SEQ
    t = jnp.arange(T)[:, None]
    s = jnp.arange(T)[None, :]
    mall = (s <= t).astype(BF16)
    x = t ^ s
    level = jnp.zeros((T, T), jnp.int32)
    for j in range(1, N_LEVELS + 1):
        level = jnp.where((x >> (j - 1)) == 1, j, level)
    level = jnp.where(s > t, -1, level)
    return mall, jnp.tile(level, (GLA_HEADS, 1))


def _permute_w_in(w, head_order):
    sizes = (FOX_W, FOX_W, FOX_W, FOX_HEADS, FOX_W, GLA_KW, GLA_KW, GLA_W, GLA_RANK, GLA_W, LRU_W, LRU_W)
    offs = [0]
    for sz in sizes:
        offs.append(offs[-1] + sz)
    fq, fk, fv, ff, fg, gq, gk, gv, ga, gg, lx, lg = [w[:, offs[n]:offs[n + 1]] for n in range(len(sizes))]

    def reorder(a):
        return jnp.take(a.reshape(a.shape[0], FOX_HEADS, -1), head_order, axis=1).reshape(a.shape)

    fq, fk, fv, ff, fg = [reorder(a) for a in (fq, fk, fv, ff, fg)]
    wq =jnp.concatenate([fq * (FOX_DH ** -0.5 * LOG2E), fk, fv], axis=1).astype(BF16)
    pad = jnp.zeros((w.shape[0], LANES - FOX_HEADS - GLA_RANK), w.dtype)
    wr = jnp.concatenate([fg, gq * (GLA_DK ** -0.5), gk, gv, gg, lx, lg, ff, ga, pad], axis=1).astype(BF16)
    return wq, wr


def kernel(x, meta, pre_g, w_in, b_f, w_a2, b_a, gla_norm_g, conv_w, conv_b, w_r, b_r, w_i, b_i, lru_lambda, w_out, post_g):
    B, S, D = x.shape
    L = S + PAD
    assert D == D_MODEL and L % T_SEQ == 0 and L % TQ == 0 and L // TK <= 2 * SUBLANES
    depth = w_in.shape[0]
    dt = x.dtype
    h = jnp.concatenate([jnp.zeros((B, PAD - N_META, D), dt),
                         jnp.broadcast_to(meta.astype(dt)[None], (B, N_META, D)), x], axis=1)
    h2 = h.reshape(B * L, D)
    mall, lv = _level_constants()
    tm_in = 256
    tm_out = 512 if (B * L) % 512 == 0 else 256
    assert (B * L) % tm_in == 0

    for l in range(depth):
        head_order = jnp.argsort(b_f[l])
        wq, wr = _permute_w_in(w_in[l], head_order)
        qkv, rest = _inproj(h2, pre_g[l][None, :], wq, wr, tm_in)
        rest3 = rest.reshape(B, L, REST_W)

        wa2 = jnp.zeros((LANES, GLA_KW), F32).at[FOX_HEADS:FOX_HEADS + GLA_RANK].set(w_a2[l]).astype(BF16)
        bfp = jnp.zeros((1, LANES), F32).at[0, 0:FOX_HEADS].set(b_f[l][head_order])
        wri = jnp.concatenate([w_r[l], w_i[l]], axis=-1).astype(BF16)
        consts = [mall, lv, wa2, b_a[l][None, :], bfp, gla_norm_g[l][None, :],
                  conv_w[l], conv_b[l][None, :], wri, b_r[l][None, :], b_i[l][None, :],
                  lru_lambda[l][None, :]]
        ygla, ylru, ccol, crow = _seqmix(rest3, consts, B, L)

        wo_fox = jnp.take(w_out[l][0:FOX_W].reshape(FOX_HEADS, FOX_DH, D), head_order, axis=0).reshape(FOX_W, D)
        wo = jnp.concatenate([wo_fox, w_out[l][FOX_W:]], axis=0).astype(BF16)
        yfox = _fox(qkv.reshape(B, L, QKV_W), ccol, crow.reshape(B, FOX_HEADS // 2, 2, L), rest3, B, L)

        h2 = _outproj(yfox.reshape(B * L, FOX_W), ygla.reshape(B * L, GLA_W), ylru.reshape(B * L, LRU_W),
                      h2, wo, post_g[l][None, :], tm_out)

    return h2.reshape(B, L, D)[:, PAD:]
```

```python
import functools

import jax
import jax.numpy as jnp
from jax import lax
from jax.experimental import pallas as pl
from jax.experimental.pallas import tpu as pltpu

F32 = jnp.float32
BF16 = jnp.bfloat16

D_MODEL = 1024
N_META = 16
PAD = 128
EPS = 1e-6
BIG = 1e30

FOX_HEADS = 8
FOX_DH = 64
FOX_W = FOX_HEADS * FOX_DH
GLA_HEADS = 4
GLA_DK = 64
GLA_DV = 128
GLA_KW = GLA_HEADS * GLA_DK
GLA_W = GLA_HEADS * GLA_DV
GLA_RANK = 16
GLA_GATE_NORM = 16.0
LRU_W = 1024
LRU_BLOCKS = 8
LRU_BS = LRU_W // LRU_BLOCKS
CONV_W = 4
LRU_C = 8.0
D_MIX = FOX_W + GLA_W + LRU_W

LANES = 128
SUBLANES = 8
VMEM_LIMIT = 56 * 1024 * 1024

QKV_W = 3 * FOX_W
R_FG = 0
R_GQ = R_FG + FOX_W
R_GK = R_GQ + GLA_KW
R_GV = R_GK + GLA_KW
R_GG = R_GV + GLA_W
R_LX = R_GG + GLA_W
R_LG = R_LX + LRU_W
R_SM = R_LG + LRU_W
REST_W = R_SM + LANES

T_SEQ = 128
N_LEVELS = 7
TQ = 640
TK = 640
LOG2E = 1.4426950408889634
SKIP_LOG2 = 160.0


def _dot(a, b):
    return jnp.dot(a, b, preferred_element_type=F32)


def _dot_nt(a, b):
    return lax.dot_general(a, b, (((1,), (1,)), ((), ())), preferred_element_type=F32)


def _dot_tn(a, b):
    return lax.dot_general(a, b, (((0,), (0,)), ((), ())), preferred_element_type=F32)


def _split3(x):
    hi = x.astype(BF16)
    r = x - hi.astype(F32)
    mid = r.astype(BF16)
    lo = (r - mid.astype(F32)).astype(BF16)
    return hi, mid, lo


def _dot3(m, x):
    hi, mid, lo = _split3(x)
    return _dot(m, hi) + _dot(m, mid) + _dot(m, lo)


def _log_sigmoid(z):
    return jnp.minimum(z, 0.0) - jnp.log(1.0 + jnp.exp2(jnp.abs(z) * -LOG2E))


def _sigmoid(z):
    return 1.0 / (1.0 + jnp.exp2(z * -LOG2E))


def _silu(z):
    return z * _sigmoid(z)


def _inproj_kernel(h_ref, g_ref, wq_ref, wr_ref, qkv_ref, rest_ref):
    x = h_ref[...]
    ms = jnp.mean(x * x, axis=-1, keepdims=True)
    hn = (x * lax.rsqrt(ms + EPS) * g_ref[...]).astype(BF16)
    step = 512
    for c in range(0, QKV_W, step):
        qkv_ref[:, c:c + step] = _dot(hn, wq_ref[:, c:c + step]).astype(BF16)
    for c in range(0, REST_W, step):
        w = min(step, REST_W - c)
        rest_ref[:, c:c + w] = _dot(hn, wr_ref[:, c:c + w])


def _inproj(h2, g, wq, wr, tm):
    n = h2.shape[0]
    return pl.pallas_call(
        _inproj_kernel,
        grid=(n // tm,),
        in_specs=[
            pl.BlockSpec((tm, D_MODEL), lambda i: (i, 0)),
            pl.BlockSpec((1, D_MODEL), lambda i: (0, 0)),
            pl.BlockSpec((D_MODEL, QKV_W), lambda i: (0, 0)),
            pl.BlockSpec((D_MODEL, REST_W), lambda i: (0, 0)),
        ],
        out_specs=[
            pl.BlockSpec((tm, QKV_W), lambda i: (i, 0)),
            pl.BlockSpec((tm, REST_W), lambda i: (i, 0)),
        ],
        out_shape=[
            jax.ShapeDtypeStruct((n, QKV_W), BF16),
            jax.ShapeDtypeStruct((n, REST_W), F32),
        ],
        compiler_params=pltpu.CompilerParams(
            dimension_semantics=("arbitrary",), vmem_limit_bytes=VMEM_LIMIT),
        name="inproj",
    )(h2, g, wq, wr)


def _level_reference(b, j, sub):
    T = b.shape[0]
    half = 1 << (j - 1)
    parts = []
    if j > 3:
        for blk in range(T >> j):
            r = (blk << j) + half - 1
            parts.append(jnp.broadcast_to(b[r:r + 1, :], (1 << j, b.shape[1])))
        return jnp.concatenate(parts, axis=0)
    for g in range(T // SUBLANES):
        x = b[g * SUBLANES:(g + 1) * SUBLANES, :]
        if j == 3:
            parts.append(jnp.broadcast_to(x[half - 1:half, :], x.shape))
            continue
        ref = x
        for off in range(-(half - 1), half + 1):
            if off != 0:
                pick = (sub & ((1 << j) - 1)) == (half - 1 + off)
                ref = jnp.where(pick, pltpu.roll(x, off % SUBLANES, 0), ref)
        parts.append(ref)
    return jnp.concatenate(parts, axis=0)


def _seqmix_kernel(gqk_ref, gv_ref, gg_ref, lx_ref, lg_ref, sm_ref,
                   mall_ref, lv_ref, wa2_ref, ba_ref, bf_ref, gng_ref,
                   cw_ref, cb_ref, wri_ref, br_ref, bi_ref, lam_ref,
                   ygla_ref, ylru_ref, ccol_ref, crow_ref,
                   st_ref, cc_ref, px_ref, hc_ref):
    t = pl.program_id(1)
    T = T_SEQ

    @pl.when(t == 0)
    def _():
        st_ref[...] = jnp.zeros_like(st_ref)
        cc_ref[...] = jnp.zeros_like(cc_ref)
        px_ref[...] = jnp.zeros_like(px_ref)
        hc_ref[...] = jnp.zeros_like(hc_ref)

    row = lax.broadcasted_iota(jnp.int32, (T, 1), 0)
    valid = ((t * T + row) >= (PAD - N_META)).astype(F32)
    tri = mall_ref[...]

    groups = T // SUBLANES
    sub = lax.broadcasted_iota(jnp.int32, (SUBLANES, 1), 0)

    def lru_stages():
        lxg = [px_ref[...]] + [lx_ref[0, g * SUBLANES:(g + 1) * SUBLANES, :] for g in range(groups)]
        px_ref[...] = lxg[groups]
        rolled = {}
        xc_parts = []
        for g in range(1, groups + 1):
            acc = lxg[g] * cw_ref[CONV_W - 1:CONV_W, :]
            for kk in range(1, CONV_W):
                for gi in (g - 1, g):
                    if (gi, kk) not in rolled:
                        rolled[(gi, kk)] = pltpu.roll(lxg[gi], kk, 0)
                shifted = jnp.where(sub >= kk, rolled[(g, kk)], rolled[(g - 1, kk)])
                acc = acc + shifted * cw_ref[CONV_W - 1 - kk:CONV_W - kk, :]
            xc_parts.append(acc)
            if g % 4 == 0:
                yield
        xc = (jnp.concatenate(xc_parts, axis=0) + cb_ref[...]) * valid
        xcb = xc.astype(BF16)
        soft = jnp.maximum(-lam_ref[...], 0.0) + jnp.log(1.0 + jnp.exp(-jnp.abs(lam_ref[...])))
        a_parts, u_parts = [], []
        for blk in range(LRU_BLOCKS):
            sl = slice(blk * LRU_BS, (blk + 1) * LRU_BS)
            ri = _dot(xcb[:, sl], wri_ref[blk])
            r = _sigmoid(ri[:, 0:LRU_BS] + br_ref[:, sl])
            ig = _sigmoid(ri[:, LRU_BS:2 * LRU_BS] + bi_ref[:, sl])
            la = -LRU_C * r * soft[:, sl]
            a_parts.append(jnp.exp(la))
            th = jnp.tanh(la)
            w = -2.0 * th / (1.0 - th)
            root = jnp.where(w > 0.0, w * lax.rsqrt(w), 0.0)
            u_parts.append(root * (ig * xc[:, sl]))
            yield
        a = jnp.concatenate(a_parts, axis=1)
        u = jnp.concatenate(u_parts, axis=1)
        h_prev = hc_ref[...]
        lg = lg_ref[0]
        for g in range(groups):
            rows = slice(g * SUBLANES, (g + 1) * SUBLANES)
            ag, ug = a[rows, :], u[rows, :]
            d = 1
            while d < SUBLANES:
                keep = sub >= d
                a_sh = jnp.where(keep, pltpu.roll(ag, d, 0), 1.0)
                u_sh = jnp.where(keep, pltpu.roll(ug, d, 0), 0.0)
                ug = ag * u_sh + ug
                ag = ag * a_sh
                d *= 2
            hg = ag * h_prev + ug
            h_prev = hg[SUBLANES - 1:SUBLANES, :]
            ylru_ref[0, rows, :] = (hg * _silu(lg[rows, :]) * valid[rows, :]).astype(BF16)
            if g % 4 == 3:
                yield
        hc_ref[...] = h_prev

    def gla_stages():
        sm = sm_ref[0]
        log_f = _log_sigmoid(sm + bf_ref[...]) * valid
        z = _dot(sm.astype(BF16), wa2_ref[...]) + ba_ref[...]
        log_a = _log_sigmoid(z) / GLA_GATE_NORM
        yield
        c = _dot3(tri, log_f) + cc_ref[...]
        cc_ref[...] = c[T - 1:T, :]
        c2 = c * LOG2E
        ccol_ref[0] = c2
        c_keys = jnp.where(valid > 0.0, c2, BIG)
        crow_ref[0] = c_keys.T[0:FOX_HEADS, :]
        b = _dot3(tri, log_a)
        yield
        gqk = gqk_ref[0]
        q = gqk[:, 0:GLA_KW]
        k = gqk[:, GLA_KW:2 * GLA_KW] * valid
        v = gv_ref[0].astype(BF16)
        lane_k = lax.broadcasted_iota(jnp.int32, (1, GLA_KW), 1)
        head_masks = [(lane_k >= hh * GLA_DK) & (lane_k < (hh + 1) * GLA_DK) for hh in range(GLA_HEADS)]

        def stack_heads(x):
            return jnp.concatenate([jnp.where(mk, x, 0.0).astype(BF16) for mk in head_masks], axis=0)

        lv = lv_ref[...]
        a_st = jnp.where(lv == 0, _dot_nt(stack_heads(q), k.astype(BF16)), 0.0)
        yield
        for j in range(1, N_LEVELS + 1):
            bref = _level_reference(b, j, sub)
            e = jnp.exp2(jnp.abs(b - bref) * -LOG2E)
            a_st = a_st + jnp.where(lv == j, _dot_nt(stack_heads(q * e), (k * e).astype(BF16)), 0.0)
            yield
        a_st = a_st.astype(BF16)
        b_last = b[T - 1:T, :]
        qd_st = stack_heads(q * jnp.exp(b))
        kd = (k * jnp.exp(b_last - b)).astype(BF16)
        st = st_ref[...]
        st_bf = st.astype(BF16)
        st_new = st * jnp.exp(b_last)
        gg = gg_ref[0]
        for hh in range(GLA_HEADS):
            vh = v[:, hh * GLA_DV:(hh + 1) * GLA_DV]
            o = _dot(a_st[hh * T:(hh + 1) * T], vh) + _dot_nt(qd_st[hh * T:(hh + 1) * T], st_bf)
            o = o * lax.rsqrt(jnp.mean(o * o, axis=-1, keepdims=True) + EPS)
            o = o * gng_ref[:, hh * GLA_DV:(hh + 1) * GLA_DV]
            y = o * _silu(gg[:, hh * GLA_DV:(hh + 1) * GLA_DV]) * valid
            ygla_ref[0, :, hh * GLA_DV:(hh + 1) * GLA_DV] = y.astype(BF16)
            st_new = st_new + jnp.where(head_masks[hh], _dot_tn(vh, kd), 0.0)
            yield
        st_ref[...] = st_new

    streams = [gla_stages(), lru_stages()]
    while streams:
        for gen in list(streams):
            if next(gen, StopIteration) is StopIteration:
                streams.remove(gen)


def _seqmix(rest3, consts, B, L):
    T = T_SEQ
    nt = L // T
    const_specs = [pl.BlockSpec(c.shape, functools.partial(lambda nd, b, t: (0,) * nd, c.ndim)) for c in consts]
    return pl.pallas_call(
        _seqmix_kernel,
        grid=(B, nt),
        in_specs=[
            pl.BlockSpec((1, T, 2 * GLA_KW), lambda b, t: (b, t, R_GQ // (2 * GLA_KW))),
            pl.BlockSpec((1, T, GLA_W), lambda b, t: (b, t, R_GV // GLA_W)),
            pl.BlockSpec((1, T, GLA_W), lambda b, t: (b, t, R_GG // GLA_W)),
            pl.BlockSpec((1, T, LRU_W), lambda b, t: (b, t, R_LX // LRU_W)),
            pl.BlockSpec((1, T, LRU_W), lambda b, t: (b, t, R_LG // LRU_W)),
            pl.BlockSpec((1, T, LANES), lambda b, t: (b, t, R_SM // LANES)),
        ] + const_specs,
        out_specs=[
            pl.BlockSpec((1, T, GLA_W), lambda b, t: (b, t, 0)),
            pl.BlockSpec((1, T, LRU_W), lambda b, t: (b, t, 0)),
            pl.BlockSpec((1, T, LANES), lambda b, t: (b, t, 0)),
            pl.BlockSpec((1, FOX_HEADS, T), lambda b, t: (b, 0, t)),
        ],
        out_shape=[
            jax.ShapeDtypeStruct((B, L, GLA_W), BF16),
            jax.ShapeDtypeStruct((B, L, LRU_W), BF16),
            jax.ShapeDtypeStruct((B, L, LANES), F32),
            jax.ShapeDtypeStruct((B, FOX_HEADS, L), F32),
        ],
        scratch_shapes=[
            pltpu.VMEM((GLA_DV, GLA_KW), F32),
            pltpu.VMEM((1, LANES), F32),
            pltpu.VMEM((SUBLANES, LRU_W), F32),
            pltpu.VMEM((1, LRU_W), F32),
        ],
        compiler_params=pltpu.CompilerParams(
            dimension_semantics=("arbitrary", "arbitrary"), vmem_limit_bytes=VMEM_LIMIT),
        name="seqmix",
    )(rest3, rest3, rest3, rest3, rest3, rest3, *consts)


def _head_sqnorm_max(x, first):
    x2 = x.astype(F32)
    x2 = x2 * x2
    n0 = jnp.sum(jnp.where(first, x2, 0.0), axis=-1, keepdims=True)
    n1 = jnp.sum(jnp.where(first, 0.0, x2), axis=-1, keepdims=True)
    return jnp.max(n0, axis=0, keepdims=True), jnp.max(n1, axis=0, keepdims=True)


def _fox_kernel(q_ref, k_ref, v_ref, ccol_ref, crow_ref, fg_ref, o_ref, kst_ref, sa_ref, sb_ref, m_ref, acc_ref):
    hp = pl.program_id(1)
    i = pl.program_id(2)
    nkb = k_ref.shape[1] // TK
    lane = lax.broadcasted_iota(jnp.int32, (1, LANES), 1)
    first = lane < FOX_DH

    @pl.when(i == 0)
    def _():
        kst_ref[...] = jnp.zeros_like(kst_ref)
        for j in range(nkb):
            n0, n1 = _head_sqnorm_max(k_ref[0, j * TK:(j + 1) * TK, :], first)
            cmin = jnp.min(crow_ref[0, 0, :, j * TK:(j + 1) * TK], axis=-1, keepdims=True)
            kst_ref[j:j + 1, :] = jnp.where(lane == 0, n0, jnp.where(lane == 1, n1, jnp.where(
                lane == 2, cmin[0:1], jnp.where(lane == 3, cmin[1:2], 0.0))))

    q = q_ref[0]
    zero = jnp.zeros_like(q)
    qh = (jnp.where(first, q, zero), jnp.where(first, zero, q))
    ccol = ccol_ref[0]
    cq = tuple(jnp.sum(jnp.where(lane == 2 * hp + hh, ccol, 0.0), axis=-1, keepdims=True) for hh in range(2))
    lane2 = lax.broadcasted_iota(jnp.int32, (1, 2 * LANES), 1)
    sel0 = (lane2 < FOX_DH) | (lane2 == LANES)
    ones_blk = jnp.broadcast_to((lane < 2).astype(BF16), (TK, LANES))

    def scores(j, dst):
        off = pl.multiple_of(j * TK, LANES)
        k = k_ref[0, pl.ds(off, TK), :]
        ck = crow_ref[0, 0, :, pl.ds(off, TK)]
        for hh in range(2):
            dst[hh] = _dot_nt(qh[hh], k) + cq[hh] - ck[hh:hh + 1, :]

    def consume(j, src, diag):
        off = pl.multiple_of(j * TK, LANES)
        vext = jnp.concatenate([v_ref[0, pl.ds(off, TK), :], ones_blk], axis=1)
        alphas, pvs = [], []
        for hh in range(2):
            s = src[hh]
            if diag:
                r_i = lax.broadcasted_iota(jnp.int32, (TQ, TK), 0)
                c_i = lax.broadcasted_iota(jnp.int32, (TQ, TK), 1)
                s = jnp.where(c_i <= r_i, s, -BIG)
            m_old = m_ref[hh]
            m_new = jnp.maximum(m_old, jnp.max(s, axis=-1, keepdims=True))
            m_ref[hh] = m_new
            alphas.append(jnp.exp2(m_old - m_new))
            pvs.append(_dot(jnp.exp2(s - m_new).astype(BF16), vext))
        acc_ref[...] = acc_ref[...] * jnp.where(sel0, alphas[0], alphas[1]) + jnp.where(sel0, pvs[0], pvs[1])

    m_ref[...] = jnp.full(m_ref.shape, -BIG, F32)
    acc_ref[...] = jnp.zeros(acc_ref.shape, F32)
    scores(i, sa_ref)

    qn = _head_sqnorm_max(q, first)
    kst = kst_ref[...]
    rowj = lax.broadcasted_iota(jnp.int32, (kst.shape[0], 1), 0)
    starts = []
    for hh in range(2):
        kn = kst[:, hh:hh + 1] * 1.01
        cmin = kst[:, 2 + hh:3 + hh]
        kn_i = jnp.sum(jnp.where(rowj == i, kn, 0.0), axis=0, keepdims=True)
        ub = jnp.max(cq[hh], axis=0, keepdims=True) - cmin + jnp.sqrt(qn[hh] * kn) + jnp.sqrt(qn[hh] * kn_i)
        skippable = (ub <= -SKIP_LOG2) & (rowj < i)
        starts.append(jnp.min(jnp.where(skippable, kst.shape[0], rowj)))
    j0 = jnp.minimum(starts[0], starts[1])
    nb = i - j0

    scores(j0, sb_ref)
    consume(i, sa_ref, True)

    def pair(n, carry):
        ja = j0 + 2 * n
        scores(ja + 1, sa_ref)
        consume(ja, sb_ref, False)
        scores(ja + 2, sb_ref)
        consume(ja + 1, sa_ref, False)
        return carry

    npairs = jnp.maximum(nb - 1, 0) // 2
    lax.fori_loop(0, npairs, pair, 0)

    @pl.when((nb > 0) & (nb % 2 == 0))
    def _():
        scores(i - 1, sa_ref)
        consume(i - 2, sb_ref, False)
        consume(i - 1, sa_ref, False)

    @pl.when(nb % 2 == 1)
    def _():
        consume(i - 1, sb_ref, False)

    acc = acc_ref[...]
    o = acc[:, 0:LANES] / jnp.where(first, acc[:, LANES:LANES + 1], acc[:, LANES + 1:LANES + 2])
    row = lax.broadcasted_iota(jnp.int32, (TQ, 1), 0)
    valid = ((i * TQ + row) >= (PAD - N_META)).astype(F32)
    o_ref[0] = (o * _silu(fg_ref[0]) * valid).astype(BF16)


def _fox(qkv3, ccol, crow4, rest3, B, L):
    nq = L // TQ
    hpairs = FOX_HEADS // 2
    return pl.pallas_call(
        _fox_kernel,
        grid=(B, hpairs, nq),
        in_specs=[
            pl.BlockSpec((1, TQ, LANES), lambda b, h, i: (b, i, h)),
            pl.BlockSpec((1, L, LANES), lambda b, h, i: (b, 0, FOX_W // LANES + h)),
            pl.BlockSpec((1, L, LANES), lambda b, h, i: (b, 0, 2 * FOX_W // LANES + h)),
            pl.BlockSpec((1, TQ, LANES), lambda b, h, i: (b, i, 0)),
            pl.BlockSpec((1, 1, 2, L), lambda b, h, i: (b, h, 0, 0)),
            pl.BlockSpec((1, TQ, LANES), lambda b, h, i: (b, i, R_FG // LANES + h)),
        ],
        out_specs=pl.BlockSpec((1, TQ, LANES), lambda b, h, i: (b, i, h)),
        out_shape=jax.ShapeDtypeStruct((B, L, FOX_W), BF16),
        scratch_shapes=[
            pltpu.VMEM((2 * SUBLANES, LANES), F32),
            pltpu.VMEM((2, TQ, TK), F32),
            pltpu.VMEM((2, TQ, TK), F32),
            pltpu.VMEM((2, TQ, 1), F32),
            pltpu.VMEM((TQ, 2 * LANES), F32),
        ],
        compiler_params=pltpu.CompilerParams(
            dimension_semantics=("arbitrary", "arbitrary", "arbitrary"), vmem_limit_bytes=VMEM_LIMIT),
        name="fox",
    )(qkv3, qkv3, qkv3, ccol, crow4, rest3)


def _outproj_kernel(yf_ref, yg_ref, yl_ref, h_ref, w_ref, g_ref, o_ref):
    acc = _dot(yf_ref[...], w_ref[0:FOX_W, :])
    acc = acc + _dot(yg_ref[...], w_ref[FOX_W:FOX_W + GLA_W, :])
    acc = acc + _dot(yl_ref[...], w_ref[FOX_W + GLA_W:D_MIX, :])
    ms = jnp.mean(acc * acc, axis=-1, keepdims=True)
    o_ref[...] = h_ref[...] + acc * lax.rsqrt(ms + EPS) * g_ref[...]


def _outproj(yf, yg, yl, h2, w, g, tm):
    n = h2.shape[0]
    return pl.pallas_call(
        _outproj_kernel,
        grid=(n // tm,),
        in_specs=[
            pl.BlockSpec((tm, FOX_W), lambda i: (i, 0)),
            pl.BlockSpec((tm, GLA_W), lambda i: (i, 0)),
            pl.BlockSpec((tm, LRU_W), lambda i: (i, 0)),
            pl.BlockSpec((tm, D_MODEL), lambda i: (i, 0)),
            pl.BlockSpec((D_MIX, D_MODEL), lambda i: (0, 0)),
            pl.BlockSpec((1, D_MODEL), lambda i: (0, 0)),
        ],
        out_specs=pl.BlockSpec((tm, D_MODEL), lambda i: (i, 0)),
        out_shape=jax.ShapeDtypeStruct((n, D_MODEL), F32),
        compiler_params=pltpu.CompilerParams(
            dimension_semantics=("arbitrary",), vmem_limit_bytes=VMEM_LIMIT),
        name="outproj",
    )(yf, yg, yl, h2, w, g)


def _level_constants():
    T = T_SEQ
    t = jnp.arange(T)[:, None]
    s = jnp.arange(T)[None, :]
    mall = (s <= t).astype(BF16)
    x = t ^ s
    level = jnp.zeros((T, T), jnp.int32)
    for j in range(1, N_LEVELS + 1):
        level = jnp.where((x >> (j - 1)) == 1, j, level)
    level = jnp.where(s > t, -1, level)
    return mall, jnp.tile(level, (GLA_HEADS, 1))


def _permute_w_in(w, head_order):
    sizes = (FOX_W, FOX_W, FOX_W, FOX_HEADS, FOX_W, GLA_KW, GLA_KW, GLA_W, GLA_RANK, GLA_W, LRU_W, LRU_W)
    offs = [0]
    for sz in sizes:
        offs.append(offs[-1] + sz)
    fq, fk, fv, ff, fg, gq, gk, gv, ga, gg, lx, lg = [w[:, offs[n]:offs[n + 1]] for n in range(len(sizes))]

    def reorder(a):
        return jnp.take(a.reshape(a.shape[0], FOX_HEADS, -1), head_order, axis=1).reshape(a.shape)

    fq, fk, fv, ff, fg = [reorder(a) for a in (fq, fk, fv, ff, fg)]
    wq = jnp.concatenate([fq * (FOX_DH ** -0.5 * LOG2E), fk, fv], axis=1).astype(BF16)
    pad = jnp.zeros((w.shape[0], LANES - FOX_HEADS - GLA_RANK), w.dtype)
    wr = jnp.concatenate([fg, gq * (GLA_DK ** -0.5), gk, gv, gg, lx, lg, ff, ga, pad], axis=1).astype(BF16)
    return wq, wr


def kernel(x, meta, pre_g, w_in, b_f, w_a2, b_a, gla_norm_g, conv_w, conv_b, w_r, b_r, w_i, b_i, lru_lambda, w_out, post_g):
    B, S, D = x.shape
    L = S + PAD
    assert D == D_MODEL and L % T_SEQ == 0 and L % TQ == 0 and L // TK <= 2 * SUBLANES
    depth = w_in.shape[0]
    dt = x.dtype
    h = jnp.concatenate([jnp.zeros((B, PAD - N_META, D), dt),
                         jnp.broadcast_to(meta.astype(dt)[None], (B, N_META, D)), x], axis=1)
    h2 = h.reshape(B * L, D)
    mall, lv = _level_constants()
    tm_in = 256
    tm_out = 512 if (B * L) % 512 == 0 else 256
    assert (B * L) % tm_in == 0

    for l in range(depth):
        head_order = jnp.argsort(b_f[l])
        wq, wr = _permute_w_in(w_in[l], head_order)
        qkv, rest = _inproj(h2, pre_g[l][None, :], wq, wr, tm_in)
        rest3 = rest.reshape(B, L, REST_W)

        wa2 = jnp.zeros((LANES, GLA_KW), F32).at[FOX_HEADS:FOX_HEADS + GLA_RANK].set(w_a2[l]).astype(BF16)
        bfp = jnp.zeros((1, LANES), F32).at[0, 0:FOX_HEADS].set(b_f[l][head_order])
        wri = jnp.concatenate([w_r[l], w_i[l]], axis=-1).astype(BF16)
        consts = [mall, lv, wa2, b_a[l][None, :], bfp, gla_norm_g[l][None, :],
                  conv_w[l], conv_b[l][None, :], wri, b_r[l][None, :], b_i[l][None, :],
                  lru_lambda[l][None, :]]
        ygla, ylru, ccol, crow = _seqmix(rest3, consts, B, L)

        wo_fox = jnp.take(w_out[l][0:FOX_W].reshape(FOX_HEADS, FOX_DH, D), head_order, axis=0).reshape(FOX_W, D)
        wo = jnp.concatenate([wo_fox, w_out[l][FOX_W:]], axis=0).astype(BF16)
        yfox = _fox(qkv.reshape(B, L, QKV_W), ccol, crow.reshape(B, FOX_HEADS // 2, 2, L), rest3, B, L)

        h2 = _outproj(yfox.reshape(B * L, FOX_W), ygla.reshape(B * L, GLA_W), ylru.reshape(B * L, LRU_W),
                      h2, wo, post_g[l][None, :], tm_out)

    return h2.reshape(B, L, D)[:, PAD:]
```

```python
import functools

import jax
import jax.numpy as jnp
from jax import lax
from jax.experimental import pallas as pl
from jax.experimental.pallas import tpu as pltpu

F32 = jnp.float32
BF16 = jnp.bfloat16

D_MODEL = 1024
N_META = 16
PAD = 128
EPS = 1e-6
BIG = 1e30

FOX_HEADS = 8
FOX_DH = 64
FOX_W = FOX_HEADS * FOX_DH
GLA_HEADS = 4
GLA_DK = 64
GLA_DV = 128
GLA_KW = GLA_HEADS * GLA_DK
GLA_W = GLA_HEADS * GLA_DV
GLA_RANK = 16
GLA_GATE_NORM = 16.0
LRU_W = 1024
LRU_BLOCKS = 8
LRU_BS = LRU_W // LRU_BLOCKS
CONV_W = 4
LRU_C = 8.0
D_MIX = FOX_W + GLA_W + LRU_W

LANES = 128
SUBLANES = 8
VMEM_LIMIT = 56 * 1024 * 1024

QKV_W = 3 * FOX_W
R_FG = 0
R_GQ = R_FG + FOX_W
R_GK = R_GQ + GLA_KW
R_GV = R_GK + GLA_KW
R_GG = R_GV + GLA_W
R_LX = R_GG + GLA_W
R_LG = R_LX + LRU_W
R_SM = R_LG + LRU_W
REST_W = R_SM + LANES

T_SEQ = 128
N_LEVELS = 7
TQ = 640
TK = 640
LOG2E = 1.4426950408889634
OUT_PARTS = 4
SKIP_LOG2 = 160.0


def _dot(a, b):
    return jnp.dot(a, b, preferred_element_type=F32)


def _dot_nt(a, b):
    return lax.dot_general(a, b, (((1,), (1,)), ((), ())), preferred_element_type=F32)


def _dot_tn(a, b):
    return lax.dot_general(a, b, (((0,), (0,)), ((), ())), preferred_element_type=F32)


def _split3(x):
    hi = x.astype(BF16)
    r = x - hi.astype(F32)
    mid = r.astype(BF16)
    lo = (r - mid.astype(F32)).astype(BF16)
    return hi, mid, lo


def _dot3(m, x):
    hi, mid, lo = _split3(x)
    return _dot(m, hi) + _dot(m, mid) + _dot(m, lo)


def _log_sigmoid(z):
    return jnp.minimum(z, 0.0) - jnp.log(1.0 + jnp.exp2(jnp.abs(z) * -LOG2E))


def _sigmoid(z):
    return 1.0 / (1.0 + jnp.exp2(z * -LOG2E))


def _silu(z):
    return z * _sigmoid(z)


def _inproj_kernel(h_ref, g_ref, w_ref, qkv_ref, rest_ref):
    x = h_ref[...]
    ms = jnp.mean(x * x, axis=-1, keepdims=True)
    hn = (x * lax.rsqrt(ms + EPS) * g_ref[0]).astype(BF16)
    step = 512
    for c in range(0, QKV_W, step):
        qkv_ref[:, c:c + step] = _dot(hn, w_ref[0, :, c:c + step]).astype(BF16)
    for c in range(0, REST_W, step):
        w = min(step, REST_W - c)
        rest_ref[:, c:c + w] = _dot(hn, w_ref[0, :, QKV_W + c:QKV_W + c + w])


def _inproj(h2, g_all, w_all, layer, tm):
    n = h2.shape[0]
    return pl.pallas_call(
        _inproj_kernel,
        grid=(n // tm,),
        in_specs=[
            pl.BlockSpec((tm, D_MODEL), lambda i: (i, 0)),
            pl.BlockSpec((1, 1, D_MODEL), lambda i: (layer, 0, 0)),
            pl.BlockSpec((1, D_MODEL, QKV_W + REST_W), lambda i: (layer, 0, 0)),
        ],
        out_specs=[
            pl.BlockSpec((tm, QKV_W), lambda i: (i, 0)),
            pl.BlockSpec((tm, REST_W), lambda i: (i, 0)),
        ],
        out_shape=[
            jax.ShapeDtypeStruct((n, QKV_W), BF16),
            jax.ShapeDtypeStruct((n, REST_W), F32),
        ],
        compiler_params=pltpu.CompilerParams(
            dimension_semantics=("arbitrary",), vmem_limit_bytes=VMEM_LIMIT),
        name="inproj",
    )(h2, g_all, w_all)


def _level_reference(b, j, sub):
    T = b.shape[0]
    half = 1 << (j - 1)
    parts = []
    if j > 3:
        for blk in range(T >> j):
            r = (blk << j) + half - 1
            parts.append(jnp.broadcast_to(b[r:r + 1, :], (1 << j, b.shape[1])))
        return jnp.concatenate(parts, axis=0)
    for g in range(T // SUBLANES):
        x = b[g * SUBLANES:(g + 1) * SUBLANES, :]
        if j == 3:
            parts.append(jnp.broadcast_to(x[half - 1:half, :], x.shape))
            continue
        ref = x
        for off in range(-(half - 1), half + 1):
            if off != 0:
                pick = (sub & ((1 << j) - 1)) == (half - 1 + off)
                ref = jnp.where(pick, pltpu.roll(x, off % SUBLANES, 0), ref)
        parts.append(ref)
    return jnp.concatenate(parts, axis=0)


def _seqmix_kernel(gqk_ref, gv_ref, gg_ref, lx_ref, lg_ref, sm_ref,
                   mall_ref, lv_ref, wa2_ref, ba_ref, bf_ref, gng_ref,
                   cw_ref, cb_ref, wri_ref, br_ref, bi_ref, lam_ref,
                   ygla_ref, ylru_ref, ccol_ref, crow_ref,
                   st_ref, cc_ref, px_ref, hc_ref):
    t = pl.program_id(1)
    T = T_SEQ

    @pl.when(t == 0)
    def _():
        st_ref[...] = jnp.zeros_like(st_ref)
        cc_ref[...] = jnp.zeros_like(cc_ref)
        px_ref[...] = jnp.zeros_like(px_ref)
        hc_ref[...] = jnp.zeros_like(hc_ref)

    row = lax.broadcasted_iota(jnp.int32, (T, 1), 0)
    valid = ((t * T + row) >= (PAD - N_META)).astype(F32)
    tri = mall_ref[...]

    groups = T // SUBLANES
    sub = lax.broadcasted_iota(jnp.int32, (SUBLANES, 1), 0)

    def lru_stages():
        lxg = [px_ref[...]] + [lx_ref[0, g * SUBLANES:(g + 1) * SUBLANES, :] for g in range(groups)]
        px_ref[...] = lxg[groups]
        rolled = {}
        xc_parts = []
        for g in range(1, groups + 1):
            acc = lxg[g] * cw_ref[0, CONV_W - 1:CONV_W, :]
            for kk in range(1, CONV_W):
                for gi in (g - 1, g):
                    if (gi, kk) not in rolled:
                        rolled[(gi, kk)] = pltpu.roll(lxg[gi], kk, 0)
                shifted = jnp.where(sub >= kk, rolled[(g, kk)], rolled[(g - 1, kk)])
                acc = acc + shifted * cw_ref[0, CONV_W - 1 - kk:CONV_W - kk, :]
            xc_parts.append(acc)
            if g % 4 == 0:
                yield
        xc = (jnp.concatenate(xc_parts, axis=0) + cb_ref[0]) * valid
        xcb = xc.astype(BF16)
        soft = jnp.maximum(-lam_ref[0], 0.0) + jnp.log(1.0 + jnp.exp(-jnp.abs(lam_ref[0])))
        a_parts, u_parts = [], []
        for blk in range(LRU_BLOCKS):
            sl = slice(blk * LRU_BS, (blk + 1) * LRU_BS)
            ri = _dot(xcb[:, sl], wri_ref[0, blk])
            r = _sigmoid(ri[:, 0:LRU_BS] + br_ref[0, :, sl])
            ig = _sigmoid(ri[:, LRU_BS:2 * LRU_BS] + bi_ref[0, :, sl])
            la = -LRU_C * r * soft[:, sl]
            a_parts.append(jnp.exp(la))
            th = jnp.tanh(la)
            w = -2.0 * th / (1.0 - th)
            root = jnp.where(w > 0.0, w * lax.rsqrt(w), 0.0)
            u_parts.append(root * (ig * xc[:, sl]))
            yield
        a = jnp.concatenate(a_parts, axis=1)
        u = jnp.concatenate(u_parts, axis=1)
        h_prev = hc_ref[...]
        lg = lg_ref[0]
        for g in range(groups):
            rows = slice(g * SUBLANES, (g + 1) * SUBLANES)
            ag, ug = a[rows, :], u[rows, :]
            d = 1
            while d < SUBLANES:
                keep = sub >= d
                a_sh = jnp.where(keep, pltpu.roll(ag, d, 0), 1.0)
                u_sh = jnp.where(keep, pltpu.roll(ug, d, 0), 0.0)
                ug = ag * u_sh + ug
                ag = ag * a_sh
                d *= 2
            hg = ag * h_prev + ug
            h_prev = hg[SUBLANES - 1:SUBLANES, :]
            ylru_ref[0, rows, :] = (hg * _silu(lg[rows, :]) * valid[rows, :]).astype(BF16)
            if g % 4 == 3:
                yield
        hc_ref[...] = h_prev

    def gla_stages():
        sm = sm_ref[0]
        log_f = _log_sigmoid(sm + bf_ref[0]) * valid
        z = _dot(sm.astype(BF16), wa2_ref[0]) + ba_ref[0]
        log_a = _log_sigmoid(z) / GLA_GATE_NORM
        yield
        c = _dot3(tri, log_f) + cc_ref[...]
        cc_ref[...] = c[T - 1:T, :]
        c2 = c * LOG2E
        ccol_ref[0] = c2
        c_keys = jnp.where(valid > 0.0, c2, BIG)
        crow_ref[0] = c_keys.T[0:FOX_HEADS, :]
        b = _dot3(tri, log_a)
        yield
        gqk = gqk_ref[0]
        q = gqk[:, 0:GLA_KW]
        k = gqk[:, GLA_KW:2 * GLA_KW] * valid
        v = gv_ref[0].astype(BF16)
        lane_k = lax.broadcasted_iota(jnp.int32, (1, GLA_KW), 1)
        head_masks = [(lane_k >= hh * GLA_DK) & (lane_k < (hh + 1) * GLA_DK) for hh in range(GLA_HEADS)]

        def stack_heads(x):
            return jnp.concatenate([jnp.where(mk, x, 0.0).astype(BF16) for mk in head_masks], axis=0)

        lv = lv_ref[...]
        a_st = jnp.where(lv == 0, _dot_nt(stack_heads(q), k.astype(BF16)), 0.0)
        yield
        for j in range(1, N_LEVELS + 1):
            bref = _level_reference(b, j, sub)
            e = jnp.exp2(jnp.abs(b - bref) * -LOG2E)
            a_st = a_st + jnp.where(lv == j, _dot_nt(stack_heads(q * e), (k * e).astype(BF16)), 0.0)
            yield
        a_st = a_st.astype(BF16)
        b_last = b[T - 1:T, :]
        qd_st = stack_heads(q * jnp.exp(b))
        kd = (k * jnp.exp(b_last - b)).astype(BF16)
        st = st_ref[...]
        st_bf = st.astype(BF16)
        st_new = st * jnp.exp(b_last)
        gg = gg_ref[0]
        for hh in range(GLA_HEADS):
            vh = v[:, hh * GLA_DV:(hh + 1) * GLA_DV]
            o = _dot(a_st[hh * T:(hh + 1) * T], vh) + _dot_nt(qd_st[hh * T:(hh + 1) * T], st_bf)
            o = o * lax.rsqrt(jnp.mean(o * o, axis=-1, keepdims=True) + EPS)
            o = o * gng_ref[0, :, hh * GLA_DV:(hh + 1) * GLA_DV]
            y = o * _silu(gg[:, hh * GLA_DV:(hh + 1) * GLA_DV]) * valid
            ygla_ref[0, :, hh * GLA_DV:(hh + 1) * GLA_DV] = y.astype(BF16)
            st_new = st_new + jnp.where(head_masks[hh], _dot_tn(vh, kd), 0.0)
            yield
        st_ref[...] = st_new

    streams = [gla_stages(), lru_stages()]
    while streams:
        for gen in list(streams):
            if next(gen, StopIteration) is StopIteration:
                streams.remove(gen)


def _seqmix(rest3, shared, stacked, layer, B, L):
    T = T_SEQ
    nt = L // T
    consts = list(shared) + list(stacked)
    const_specs = [pl.BlockSpec(c.shape, functools.partial(lambda nd, b, t: (0,) * nd, c.ndim)) for c in shared]
    const_specs += [pl.BlockSpec((1,) + c.shape[1:], functools.partial(lambda nd, b, t: (layer,) + (0,) * nd, c.ndim - 1))
                    for c in stacked]
    return pl.pallas_call(
        _seqmix_kernel,
        grid=(B, nt),
        in_specs=[
            pl.BlockSpec((1, T, 2 * GLA_KW), lambda b, t: (b, t, R_GQ // (2 * GLA_KW))),
            pl.BlockSpec((1, T, GLA_W), lambda b, t: (b, t, R_GV // GLA_W)),
            pl.BlockSpec((1, T, GLA_W), lambda b, t: (b, t, R_GG // GLA_W)),
            pl.BlockSpec((1, T, LRU_W), lambda b, t: (b, t, R_LX // LRU_W)),
            pl.BlockSpec((1, T, LRU_W), lambda b, t: (b, t, R_LG // LRU_W)),
            pl.BlockSpec((1, T, LANES), lambda b, t: (b, t, R_SM // LANES)),
        ] + const_specs,
        out_specs=[
            pl.BlockSpec((1, T, GLA_W), lambda b, t: (b, t, 0)),
            pl.BlockSpec((1, T, LRU_W), lambda b, t: (b, t, 0)),
            pl.BlockSpec((1, T, LANES), lambda b, t: (b, t, 0)),
            pl.BlockSpec((1, FOX_HEADS, T), lambda b, t: (b, 0, t)),
        ],
        out_shape=[
            jax.ShapeDtypeStruct((B, L, GLA_W), BF16),
            jax.ShapeDtypeStruct((B, L, LRU_W), BF16),
            jax.ShapeDtypeStruct((B, L, LANES), F32),
            jax.ShapeDtypeStruct((B, FOX_HEADS, L), F32),
        ],
        scratch_shapes=[
            pltpu.VMEM((GLA_DV, GLA_KW), F32),
            pltpu.VMEM((1, LANES), F32),
            pltpu.VMEM((SUBLANES, LRU_W), F32),
            pltpu.VMEM((1, LRU_W), F32),
        ],
        compiler_params=pltpu.CompilerParams(
            dimension_semantics=("arbitrary", "arbitrary"), vmem_limit_bytes=VMEM_LIMIT),
        name="seqmix",
    )(rest3, rest3, rest3, rest3, rest3, rest3, *consts)


def _head_sqnorm_max(x, first):
    x2 = x.astype(F32)
    x2 = x2 * x2
    n0 = jnp.sum(jnp.where(first, x2, 0.0), axis=-1, keepdims=True)
    n1 = jnp.sum(jnp.where(first, 0.0, x2), axis=-1, keepdims=True)
    return jnp.max(n0, axis=0, keepdims=True), jnp.max(n1, axis=0, keepdims=True)


def _fox_kernel(q_ref, k_ref, v_ref, ccol_ref, crow_ref, fg_ref, o_ref, kst_ref, sa_ref, sb_ref, m_ref, acc_ref):
    hp = pl.program_id(1)
    i = pl.program_id(2)
    nkb = k_ref.shape[1] // TK
    lane = lax.broadcasted_iota(jnp.int32, (1, LANES), 1)
    first = lane < FOX_DH

    @pl.when(i == 0)
    def _():
        kst_ref[...] = jnp.zeros_like(kst_ref)
        for j in range(nkb):
            n0, n1 = _head_sqnorm_max(k_ref[0, j * TK:(j + 1) * TK, :], first)
            cmin = jnp.min(crow_ref[0, 0, :, j * TK:(j + 1) * TK], axis=-1, keepdims=True)
            kst_ref[j:j + 1, :] = jnp.where(lane == 0, n0, jnp.where(lane == 1, n1, jnp.where(
                lane == 2, cmin[0:1], jnp.where(lane == 3, cmin[1:2], 0.0))))

    q = q_ref[0]
    zero = jnp.zeros_like(q)
    qh = (jnp.where(first, q, zero), jnp.where(first, zero, q))
    ccol = ccol_ref[0]
    cq = tuple(jnp.sum(jnp.where(lane == 2 * hp + hh, ccol, 0.0), axis=-1, keepdims=True) for hh in range(2))
    lane2 = lax.broadcasted_iota(jnp.int32, (1, 2 * LANES), 1)
    sel0 = (lane2 < FOX_DH) | (lane2 == LANES)
    ones_blk = jnp.broadcast_to((lane < 2).astype(BF16), (TK, LANES))

    def scores(j, dst):
        off = pl.multiple_of(j * TK, LANES)
        k = k_ref[0, pl.ds(off, TK), :]
        ck = crow_ref[0, 0, :, pl.ds(off, TK)]
        for hh in range(2):
            dst[hh] = _dot_nt(qh[hh], k) + cq[hh] - ck[hh:hh + 1, :]

    def consume(j, src, diag):
        off = pl.multiple_of(j * TK, LANES)
        vext = jnp.concatenate([v_ref[0, pl.ds(off, TK), :], ones_blk], axis=1)
        alphas, pvs = [], []
        for hh in range(2):
            s = src[hh]
            if diag:
                r_i = lax.broadcasted_iota(jnp.int32, (TQ, TK), 0)
                c_i = lax.broadcasted_iota(jnp.int32, (TQ, TK), 1)
                s = jnp.where(c_i <= r_i, s, -BIG)
            m_old = m_ref[hh]
            m_new = jnp.maximum(m_old, jnp.max(s, axis=-1, keepdims=True))
            m_ref[hh] = m_new
            alphas.append(jnp.exp2(m_old - m_new))
            pvs.append(_dot(jnp.exp2(s - m_new).astype(BF16), vext))
        acc_ref[...] = acc_ref[...] * jnp.where(sel0, alphas[0], alphas[1]) + jnp.where(sel0, pvs[0], pvs[1])

    m_ref[...] = jnp.full(m_ref.shape, -BIG, F32)
    acc_ref[...] = jnp.zeros(acc_ref.shape, F32)
    scores(i, sa_ref)

    qn = _head_sqnorm_max(q, first)
    kst = kst_ref[...]
    rowj = lax.broadcasted_iota(jnp.int32, (kst.shape[0], 1), 0)
    starts = []
    for hh in range(2):
        kn = kst[:, hh:hh + 1] * 1.01
        cmin = kst[:, 2 + hh:3 + hh]
        kn_i = jnp.sum(jnp.where(rowj == i, kn, 0.0), axis=0, keepdims=True)
        ub = jnp.max(cq[hh], axis=0, keepdims=True) - cmin + jnp.sqrt(qn[hh] * kn) + jnp.sqrt(qn[hh] * kn_i)
        skippable = (ub <= -SKIP_LOG2) & (rowj < i)
        starts.append(jnp.min(jnp.where(skippable, kst.shape[0], rowj)))
    j0 = jnp.minimum(starts[0], starts[1])
    nb = i - j0

    scores(j0, sb_ref)
    consume(i, sa_ref, True)

    def pair(n, carry):
        ja = j0 + 2 * n
        scores(ja + 1, sa_ref)
        consume(ja, sb_ref, False)
        scores(ja + 2, sb_ref)
        consume(ja + 1, sa_ref, False)
        return carry

    npairs = jnp.maximum(nb - 1, 0) // 2
    lax.fori_loop(0, npairs, pair, 0)

    @pl.when((nb > 0) & (nb % 2 == 0))
    def _():
        scores(i - 1, sa_ref)
        consume(i - 2, sb_ref, False)
        consume(i - 1, sa_ref, False)

    @pl.when(nb % 2 == 1)
    def _():
        consume(i - 1, sb_ref, False)

    acc = acc_ref[...]
    o = acc[:, 0:LANES] / jnp.where(first, acc[:, LANES:LANES + 1], acc[:, LANES + 1:LANES + 2])
    row = lax.broadcasted_iota(jnp.int32, (TQ, 1), 0)
    valid = ((i * TQ + row) >= (PAD - N_META)).astype(F32)
    o_ref[0] = (o * _silu(fg_ref[0]) * valid).astype(BF16)


def _fox(qkv3, ccol, crow4, rest3, B, L):
    nq = L // TQ
    hpairs = FOX_HEADS // 2
    return pl.pallas_call(
        _fox_kernel,
        grid=(B, hpairs, nq),
        in_specs=[
            pl.BlockSpec((1, TQ, LANES), lambda b, h, i: (b, i, h)),
            pl.BlockSpec((1, L, LANES), lambda b, h, i: (b, 0, FOX_W // LANES + h)),
            pl.BlockSpec((1, L, LANES), lambda b, h, i: (b, 0, 2 * FOX_W // LANES + h)),
            pl.BlockSpec((1, TQ, LANES), lambda b, h, i: (b, i, 0)),
            pl.BlockSpec((1, 1, 2, L), lambda b, h, i: (b, h, 0, 0)),
            pl.BlockSpec((1, TQ, LANES), lambda b, h, i: (b, i, R_FG // LANES + h)),
        ],
        out_specs=pl.BlockSpec((1, TQ, LANES), lambda b, h, i: (b, i, h)),
        out_shape=jax.ShapeDtypeStruct((B, L, FOX_W), BF16),
        scratch_shapes=[
            pltpu.VMEM((2 * SUBLANES, LANES), F32),
            pltpu.VMEM((2, TQ, TK), F32),
            pltpu.VMEM((2, TQ, TK), F32),
            pltpu.VMEM((2, TQ, 1), F32),
            pltpu.VMEM((TQ, 2 * LANES), F32),
        ],
        compiler_params=pltpu.CompilerParams(
            dimension_semantics=("arbitrary", "arbitrary", "arbitrary"), vmem_limit_bytes=VMEM_LIMIT),
        name="fox",
    )(qkv3, qkv3, qkv3, ccol, crow4, rest3)


def _outproj_math(yf, yg, yl, h, w_ref, g_ref):
    acc = _dot(yf, w_ref[0, 0:FOX_W, :])
    acc = acc + _dot(yg, w_ref[0, FOX_W:FOX_W + GLA_W, :])
    acc = acc + _dot(yl, w_ref[0, FOX_W + GLA_W:D_MIX, :])
    ms = jnp.mean(acc * acc, axis=-1, keepdims=True)
    return h + acc * lax.rsqrt(ms + EPS) * g_ref[0]


def _outproj_kernel(yf_ref, yg_ref, yl_ref, h_ref, w_ref, g_ref, o_ref):
    o_ref[...] = _outproj_math(yf_ref[...], yg_ref[...], yl_ref[...], h_ref[...], w_ref, g_ref)


def _outproj(yf, yg, yl, h2, w_all, g_all, layer, tm):
    n = h2.shape[0]
    return pl.pallas_call(
        _outproj_kernel,
        grid=(n // tm,),
        in_specs=[
            pl.BlockSpec((tm, FOX_W), lambda i: (i, 0)),
            pl.BlockSpec((tm, GLA_W), lambda i: (i, 0)),
            pl.BlockSpec((tm, LRU_W), lambda i: (i, 0)),
            pl.BlockSpec((tm, D_MODEL), lambda i: (i, 0)),
            pl.BlockSpec((1, D_MIX, D_MODEL), lambda i: (layer, 0, 0)),
            pl.BlockSpec((1, 1, D_MODEL), lambda i: (layer, 0, 0)),
        ],
        out_specs=pl.BlockSpec((tm, D_MODEL), lambda i: (i, 0)),
        out_shape=jax.ShapeDtypeStruct((n, D_MODEL), F32),
        compiler_params=pltpu.CompilerParams(
            dimension_semantics=("arbitrary",), vmem_limit_bytes=VMEM_LIMIT),
        name="outproj",
    )(yf, yg, yl, h2, w_all, g_all)


def _outproj_last_kernel(*refs):
    n = OUT_PARTS
    yf, yg, yl, h = [jnp.concatenate([r[0] for r in refs[a * n:(a + 1) * n]], axis=0) for a in range(4)]
    w_ref, g_ref, o_ref = refs[4 * n:]
    o_ref[0] = _outproj_math(yf, yg, yl, h, w_ref, g_ref)


def _outproj_last(yf3, yg3, yl3, h3, w_all, g_all, layer, S):
    B = h3.shape[0]
    tm = OUT_PARTS * PAD

    def part_specs(width):
        return [pl.BlockSpec((1, PAD, width), functools.partial(lambda r, b, j: (b, 1 + OUT_PARTS * j + r, 0), r))
                for r in range(OUT_PARTS)]

    in_specs = part_specs(FOX_W) + part_specs(GLA_W) + part_specs(LRU_W) + part_specs(D_MODEL) + [
        pl.BlockSpec((1, D_MIX, D_MODEL), lambda b, j: (layer, 0, 0)),
        pl.BlockSpec((1, 1, D_MODEL), lambda b, j: (layer, 0, 0)),
    ]
    args = [yf3] * OUT_PARTS + [yg3] * OUT_PARTS + [yl3] * OUT_PARTS + [h3] * OUT_PARTS + [w_all, g_all]
    return pl.pallas_call(
        _outproj_last_kernel,
        grid=(B, S // tm),
        in_specs=in_specs,
        out_specs=pl.BlockSpec((1, tm, D_MODEL), lambda b, j: (b, j, 0)),
        out_shape=jax.ShapeDtypeStruct((B, S, D_MODEL), F32),
        compiler_params=pltpu.CompilerParams(
            dimension_semantics=("arbitrary", "arbitrary"), vmem_limit_bytes=VMEM_LIMIT),
        name="outproj_last",
    )(*args)


def _level_constants():
    T = T_SEQ
    t = jnp.arange(T)[:, None]
    s = jnp.arange(T)[None, :]
    mall = (s <= t).astype(BF16)
    x = t ^ s
    level = jnp.zeros((T, T), jnp.int32)
    for j in range(1, N_LEVELS + 1):
        level = jnp.where((x >> (j - 1)) == 1, j, level)
    level = jnp.where(s > t, -1, level)
    return mall, jnp.tile(level, (GLA_HEADS, 1))


def _prepare_weights(w_in, w_out, b_f):
    depth = w_in.shape[0]
    sizes = (FOX_W, FOX_W, FOX_W, FOX_HEADS, FOX_W, GLA_KW, GLA_KW, GLA_W, GLA_RANK, GLA_W, LRU_W, LRU_W)
    names = ("fq", "fk", "fv", "ff", "fg", "gq", "gk", "gv", "ga", "gg", "lx", "lg")
    offs, acc = {}, 0
    for nm, sz in zip(names, sizes):
        offs[nm] = acc
        acc += sz
    head_order = jnp.argsort(b_f, axis=-1)

    def head_cols(nm, width):
        return (offs[nm] + head_order[:, :, None] * width + jnp.arange(width)[None, None, :]).reshape(depth, -1)

    def plain_cols(nm, width):
        return jnp.broadcast_to(offs[nm] + jnp.arange(width)[None, :], (depth, width))

    n_pad = LANES - FOX_HEADS - GLA_RANK
    cols = jnp.concatenate([
        head_cols("fq", FOX_DH), head_cols("fk", FOX_DH), head_cols("fv", FOX_DH),
        head_cols("fg", FOX_DH), plain_cols("gq", GLA_KW), plain_cols("gk", GLA_KW), plain_cols("gv", GLA_W),
        plain_cols("gg", GLA_W), plain_cols("lx", LRU_W), plain_cols("lg", LRU_W),
        head_cols("ff", 1), plain_cols("ga", GLA_RANK), jnp.zeros((depth, n_pad), jnp.int32)], axis=1)
    scale = jnp.concatenate([
        jnp.full((FOX_W,), FOX_DH ** -0.5 * LOG2E, F32), jnp.ones((2 * FOX_W + FOX_W,), F32),
        jnp.full((GLA_KW,), GLA_DK ** -0.5, F32),
        jnp.ones((GLA_KW + 2 * GLA_W + 2 * LRU_W + FOX_HEADS + GLA_RANK,), F32), jnp.zeros((n_pad,), F32)])
    w_all = (jnp.take_along_axis(w_in, cols[:, None, :], axis=2) * scale).astype(BF16)
    rows = jnp.concatenate([
        (head_order[:, :, None] * FOX_DH + jnp.arange(FOX_DH)[None, None, :]).reshape(depth, -1),
        jnp.broadcast_to(jnp.arange(FOX_W, D_MIX)[None, :], (depth, D_MIX - FOX_W))], axis=1)
    wo_all = jnp.take_along_axis(w_out, rows[:, :, None], axis=1).astype(BF16)
    return w_all, wo_all, head_order


def kernel(x, meta, pre_g, w_in, b_f, w_a2, b_a, gla_norm_g, conv_w, conv_b, w_r, b_r, w_i, b_i, lru_lambda, w_out, post_g):
    B, S, D = x.shape
    L = S + PAD
    tm_in = 512
    tm_out = 512 if (B * L) % 512 == 0 else 256
    assert D == D_MODEL and L % T_SEQ == 0 and L % TQ == 0 and L // TK <= 2 * SUBLANES
    assert (B * L) % tm_in == 0 and S % (OUT_PARTS * PAD) == 0
    depth = w_in.shape[0]
    dt = x.dtype
    h = jnp.concatenate([jnp.zeros((B, PAD - N_META, D), dt),
                         jnp.broadcast_to(meta.astype(dt)[None], (B, N_META, D)), x], axis=1)
    h2 = h.reshape(B * L, D)

    w_all, wo_all, head_order = _prepare_weights(w_in, w_out, b_f)
    row3 = lambda a: a[:, None, :]
    wa2 = jnp.zeros((depth, LANES, GLA_KW), F32).at[:, FOX_HEADS:FOX_HEADS + GLA_RANK].set(w_a2).astype(BF16)
    bfp = jnp.zeros((depth, 1, LANES), F32).at[:, 0, 0:FOX_HEADS].set(jnp.take_along_axis(b_f, head_order, axis=1))
    wri = jnp.concatenate([w_r, w_i], axis=-1).astype(BF16)
    shared = list(_level_constants())
    stacked = [wa2, row3(b_a), bfp, row3(gla_norm_g), conv_w, row3(conv_b), wri, row3(b_r), row3(b_i),
               row3(lru_lambda)]
    pre_g3, post_g3 = row3(pre_g), row3(post_g)

    out = None
    for l in range(depth):
        qkv, rest = _inproj(h2, pre_g3, w_all, l, tm_in)
        rest3 = rest.reshape(B, L, REST_W)
        ygla, ylru, ccol, crow = _seqmix(rest3, shared, stacked, l, B, L)
        yfox = _fox(qkv.reshape(B, L, QKV_W), ccol, crow.reshape(B, FOX_HEADS // 2, 2, L), rest3, B, L)
        if l + 1 < depth:
            h2 = _outproj(yfox.reshape(B * L, FOX_W), ygla.reshape(B * L, GLA_W), ylru.reshape(B * L, LRU_W),
                          h2, wo_all, post_g3, l, tm_out)
        else:
            out = _outproj_last(yfox, ygla, ylru, h2.reshape(B, L, D), wo_all, post_g3, l, S)
    return out
```

```python
import functools

import jax
import jax.numpy as jnp
from jax import lax
from jax.experimental import pallas as pl
from jax.experimental.pallas import tpu as pltpu

F32 = jnp.float32
BF16 = jnp.bfloat16

D_MODEL = 1024
N_META = 16
PAD = 128
EPS = 1e-6
BIG = 1e30

FOX_HEADS = 8
FOX_DH = 64
FOX_W = FOX_HEADS * FOX_DH
GLA_HEADS = 4
GLA_DK = 64
GLA_DV = 128
GLA_KW = GLA_HEADS * GLA_DK
GLA_W = GLA_HEADS * GLA_DV
GLA_RANK = 16
GLA_GATE_NORM = 16.0
LRU_W = 1024
LRU_BLOCKS = 8
LRU_BS = LRU_W // LRU_BLOCKS
CONV_W = 4
LRU_C = 8.0
D_MIX = FOX_W + GLA_W + LRU_W

LANES = 128
SUBLANES = 8
VMEM_LIMIT = 56 * 1024 * 1024

QKV_W = 3 * FOX_W
R_FG = 0
R_GQ = R_FG + FOX_W
R_GK = R_GQ + GLA_KW
R_GV = R_GK + GLA_KW
R_GG = R_GV + GLA_W
R_LX = R_GG + GLA_W
R_LG = R_LX + LRU_W
R_SM = R_LG + LRU_W
REST_W = R_SM + LANES

T_SEQ = 128
N_LEVELS = 7
TQ = 640
TK = 640
LOG2E = 1.4426950408889634
OUT_PARTS = 4
SKIP_LOG2 = 160.0


def _dot(a, b):
    return jnp.dot(a, b, preferred_element_type=F32)


def _dot_nt(a, b):
    return lax.dot_general(a, b, (((1,), (1,)), ((), ())), preferred_element_type=F32)


def _dot_tn(a, b):
    return lax.dot_general(a, b, (((0,), (0,)), ((), ())), preferred_element_type=F32)


def _split3(x):
    hi = x.astype(BF16)
    r = x - hi.astype(F32)
    mid = r.astype(BF16)
    lo = (r - mid.astype(F32)).astype(BF16)
    return hi, mid, lo


def _dot3(m, x):
    hi, mid, lo = _split3(x)
    return _dot(m, hi) + _dot(m, mid) + _dot(m, lo)


def _log_sigmoid(z):
    return jnp.minimum(z, 0.0) - jnp.log(1.0 + jnp.exp2(jnp.abs(z) * -LOG2E))


def _sigmoid(z):
    return 1.0 / (1.0 + jnp.exp2(z * -LOG2E))


def _silu(z):
    return z * _sigmoid(z)


def _inproj_kernel(h_ref, g_ref, w_ref, qkv_ref, rest_ref):
    x = h_ref[...]
    ms = jnp.mean(x * x, axis=-1, keepdims=True)
    hn = (x * lax.rsqrt(ms + EPS) * g_ref[0]).astype(BF16)
    step = 512
    for c in range(0, QKV_W, step):
        qkv_ref[:, c:c + step] = _dot(hn, w_ref[0, :, c:c + step]).astype(BF16)
    for c in range(0, REST_W, step):
        w = min(step, REST_W - c)
        rest_ref[:, c:c + w] = _dot(hn, w_ref[0, :, QKV_W + c:QKV_W + c + w])


def _inproj(h2, g_all, w_all, layer, tm):
    n = h2.shape[0]
    return pl.pallas_call(
        _inproj_kernel,
        grid=(n // tm,),
        in_specs=[
            pl.BlockSpec((tm, D_MODEL), lambda i: (i, 0)),
            pl.BlockSpec((1, 1, D_MODEL), lambda i: (layer, 0, 0)),
            pl.BlockSpec((1, D_MODEL, QKV_W + REST_W), lambda i: (layer, 0, 0)),
        ],
        out_specs=[
            pl.BlockSpec((tm, QKV_W), lambda i: (i, 0)),
            pl.BlockSpec((tm, REST_W), lambda i: (i, 0)),
        ],
        out_shape=[
            jax.ShapeDtypeStruct((n, QKV_W), BF16),
            jax.ShapeDtypeStruct((n, REST_W), F32),
        ],
        compiler_params=pltpu.CompilerParams(
            dimension_semantics=("arbitrary",), vmem_limit_bytes=VMEM_LIMIT),
        name="inproj",
    )(h2, g_all, w_all)


def _level_reference(b, j, sub):
    T = b.shape[0]
    half = 1 << (j - 1)
    parts = []
    if j > 3:
        for blk in range(T >> j):
            r = (blk << j) + half - 1
            parts.append(jnp.broadcast_to(b[r:r + 1, :], (1 << j, b.shape[1])))
        return jnp.concatenate(parts, axis=0)
    for g in range(T // SUBLANES):
        x = b[g * SUBLANES:(g + 1) * SUBLANES, :]
        if j == 3:
            parts.append(jnp.broadcast_to(x[half - 1:half, :], x.shape))
            continue
        ref = x
        for off in range(-(half - 1), half + 1):
            if off != 0:
                pick = (sub & ((1 << j) - 1)) == (half - 1 + off)
                ref = jnp.where(pick, pltpu.roll(x, off % SUBLANES, 0), ref)
        parts.append(ref)
    return jnp.concatenate(parts, axis=0)


def _seqmix_kernel(gqk_ref, gv_ref, gg_ref, lx_ref, lg_ref, sm_ref,
                   mall_ref, lv_ref, wa2_ref, ba_ref, bf_ref, gng_ref,
                   cw_ref, cb_ref, wri_ref, br_ref, bi_ref, lam_ref,
                   ygla_ref, ylru_ref, ccol_ref, crow_ref,
                   st_ref, cc_ref, px_ref, hc_ref):
    t = pl.program_id(1)
    T = T_SEQ

    @pl.when(t == 0)
    def _():
        st_ref[...] = jnp.zeros_like(st_ref)
        cc_ref[...] = jnp.zeros_like(cc_ref)
        px_ref[...] = jnp.zeros_like(px_ref)
        hc_ref[...] = jnp.zeros_like(hc_ref)

    row = lax.broadcasted_iota(jnp.int32, (T, 1), 0)
    valid = ((t * T + row) >= (PAD - N_META)).astype(F32)
    tri = mall_ref[...]

    groups = T // SUBLANES
    sub = lax.broadcasted_iota(jnp.int32, (SUBLANES, 1), 0)

    def lru_stages():
        lxg = [px_ref[...]] + [lx_ref[0, g * SUBLANES:(g + 1) * SUBLANES, :] for g in range(groups)]
        px_ref[...] = lxg[groups]
        rolled = {}
        xc_parts = []
        for g in range(1, groups + 1):
            acc = lxg[g] * cw_ref[0, CONV_W - 1:CONV_W, :]
            for kk in range(1, CONV_W):
                for gi in (g - 1, g):
                    if (gi, kk) not in rolled:
                        rolled[(gi, kk)] = pltpu.roll(lxg[gi], kk, 0)
                shifted = jnp.where(sub >= kk, rolled[(g, kk)], rolled[(g - 1, kk)])
                acc = acc + shifted * cw_ref[0, CONV_W - 1 - kk:CONV_W - kk, :]
            xc_parts.append(acc)
            if g % 4 == 0:
                yield
        xc = (jnp.concatenate(xc_parts, axis=0) + cb_ref[0]) * valid
        xcb = xc.astype(BF16)
        soft = jnp.maximum(-lam_ref[0], 0.0) + jnp.log(1.0 + jnp.exp(-jnp.abs(lam_ref[0])))
        a_parts, u_parts = [], []
        for blk in range(LRU_BLOCKS):
            sl = slice(blk * LRU_BS, (blk + 1) * LRU_BS)
            ri = _dot(xcb[:, sl], wri_ref[0, blk])
            r = _sigmoid(ri[:, 0:LRU_BS] + br_ref[0, :, sl])
            ig = _sigmoid(ri[:, LRU_BS:2 * LRU_BS] + bi_ref[0, :, sl])
            la = -LRU_C * r * soft[:, sl]
            a_parts.append(jnp.exp(la))
            th = jnp.tanh(la)
            w = -2.0 * th / (1.0 - th)
            root = jnp.where(w > 0.0, w * lax.rsqrt(w), 0.0)
            u_parts.append(root * (ig * xc[:, sl]))
            yield
        a = jnp.concatenate(a_parts, axis=1)
        u = jnp.concatenate(u_parts, axis=1)
        h_prev = hc_ref[...]
        lg = lg_ref[0]
        for g in range(groups):
            rows = slice(g * SUBLANES, (g + 1) * SUBLANES)
            ag, ug = a[rows, :], u[rows, :]
            d = 1
            while d < SUBLANES:
                keep = sub >= d
                a_sh = jnp.where(keep, pltpu.roll(ag, d, 0), 1.0)
                u_sh = jnp.where(keep, pltpu.roll(ug, d, 0), 0.0)
                ug = ag * u_sh + ug
                ag = ag * a_sh
                d *= 2
            hg = ag * h_prev + ug
            h_prev = hg[SUBLANES - 1:SUBLANES, :]
            ylru_ref[0, rows, :] = (hg * _silu(lg[rows, :]) * valid[rows, :]).astype(BF16)
            if g % 4 == 3:
                yield
        hc_ref[...] = h_prev

    def gla_stages():
        sm = sm_ref[0]
        log_f = _log_sigmoid(sm + bf_ref[0]) * valid
        z = _dot(sm.astype(BF16), wa2_ref[0]) + ba_ref[0]
        log_a = _log_sigmoid(z) / GLA_GATE_NORM
        yield
        c = _dot3(tri, log_f) + cc_ref[...]
        cc_ref[...] = c[T - 1:T, :]
        c2 = c * LOG2E
        ccol_ref[0] = c2
        c_keys = jnp.where(valid > 0.0, c2, BIG)
        crow_ref[0] = c_keys.T[0:FOX_HEADS, :]
        b = _dot3(tri, log_a)
        yield
        gqk = gqk_ref[0]
        q = gqk[:, 0:GLA_KW]
        k = gqk[:, GLA_KW:2 * GLA_KW] * valid
        v = gv_ref[0].astype(BF16)
        lane_k = lax.broadcasted_iota(jnp.int32, (1, GLA_KW), 1)
        head_masks = [(lane_k >= hh * GLA_DK) & (lane_k < (hh + 1) * GLA_DK) for hh in range(GLA_HEADS)]

        def stack_heads(x):
            return jnp.concatenate([jnp.where(mk, x, 0.0).astype(BF16) for mk in head_masks], axis=0)

        lv = lv_ref[...]
        a_st = jnp.where(lv == 0, _dot_nt(stack_heads(q), k.astype(BF16)), 0.0)
        yield
        for j in range(1, N_LEVELS + 1):
            bref = _level_reference(b, j, sub)
            e = jnp.exp2(jnp.abs(b - bref) * -LOG2E)
            a_st = a_st + jnp.where(lv == j, _dot_nt(stack_heads(q * e), (k * e).astype(BF16)), 0.0)
            yield
        a_st = a_st.astype(BF16)
        b_last = b[T - 1:T, :]
        qd_st = stack_heads(q * jnp.exp(b))
        kd = (k * jnp.exp(b_last - b)).astype(BF16)
        st = st_ref[...]
        st_bf = st.astype(BF16)
        st_new = st * jnp.exp(b_last)
        gg = gg_ref[0]
        for hh in range(GLA_HEADS):
            vh = v[:, hh * GLA_DV:(hh + 1) * GLA_DV]
            o = _dot(a_st[hh * T:(hh + 1) * T], vh) + _dot_nt(qd_st[hh * T:(hh + 1) * T], st_bf)
            o = o * lax.rsqrt(jnp.mean(o * o, axis=-1, keepdims=True) + EPS)
            o = o * gng_ref[0, :, hh * GLA_DV:(hh + 1) * GLA_DV]
            y = o * _silu(gg[:, hh * GLA_DV:(hh + 1) * GLA_DV]) * valid
            ygla_ref[0, :, hh * GLA_DV:(hh + 1) * GLA_DV] = y.astype(BF16)
            st_new = st_new + jnp.where(head_masks[hh], _dot_tn(vh, kd), 0.0)
            yield
        st_ref[...] = st_new

    streams = [gla_stages(), lru_stages()]
    while streams:
        for gen in list(streams):
            if next(gen, StopIteration) is StopIteration:
                streams.remove(gen)


def _seqmix(rest3, shared, stacked, layer, B, L):
    T = T_SEQ
    nt = L // T
    consts = list(shared) + list(stacked)
    const_specs = [pl.BlockSpec(c.shape, functools.partial(lambda nd, b, t: (0,) * nd, c.ndim)) for c in shared]
    const_specs += [pl.BlockSpec((1,) + c.shape[1:], functools.partial(lambda nd, b, t: (layer,) + (0,) * nd, c.ndim - 1))
                    for c in stacked]
    return pl.pallas_call(
        _seqmix_kernel,
        grid=(B, nt),
        in_specs=[
            pl.BlockSpec((1, T, 2 * GLA_KW), lambda b, t: (b, t, R_GQ // (2 * GLA_KW))),
            pl.BlockSpec((1, T, GLA_W), lambda b, t: (b, t, R_GV // GLA_W)),
            pl.BlockSpec((1, T, GLA_W), lambda b, t: (b, t, R_GG // GLA_W)),
            pl.BlockSpec((1, T, LRU_W), lambda b, t: (b, t, R_LX // LRU_W)),
            pl.BlockSpec((1, T, LRU_W), lambda b, t: (b, t, R_LG // LRU_W)),
            pl.BlockSpec((1, T, LANES), lambda b, t: (b, t, R_SM // LANES)),
        ] + const_specs,
        out_specs=[
            pl.BlockSpec((1, T, GLA_W), lambda b, t: (b, t, 0)),
            pl.BlockSpec((1, T, LRU_W), lambda b, t: (b, t, 0)),
            pl.BlockSpec((1, T, LANES), lambda b, t: (b, t, 0)),
            pl.BlockSpec((1, FOX_HEADS, T), lambda b, t: (b, 0, t)),
        ],
        out_shape=[
            jax.ShapeDtypeStruct((B, L, GLA_W), BF16),
            jax.ShapeDtypeStruct((B, L, LRU_W), BF16),
            jax.ShapeDtypeStruct((B, L, LANES), F32),
            jax.ShapeDtypeStruct((B, FOX_HEADS, L), F32),
        ],
        scratch_shapes=[
            pltpu.VMEM((GLA_DV, GLA_KW), F32),
            pltpu.VMEM((1, LANES), F32),
            pltpu.VMEM((SUBLANES, LRU_W), F32),
            pltpu.VMEM((1, LRU_W), F32),
        ],
        compiler_params=pltpu.CompilerParams(
            dimension_semantics=("arbitrary", "arbitrary"), vmem_limit_bytes=VMEM_LIMIT),
        name="seqmix",
    )(rest3, rest3, rest3, rest3, rest3, rest3, *consts)


def _head_sqnorm_max(x, first):
    x2 = x.astype(F32)
    x2 = x2 * x2
    n0 = jnp.sum(jnp.where(first, x2, 0.0), axis=-1, keepdims=True)
    n1 = jnp.sum(jnp.where(first, 0.0, x2), axis=-1, keepdims=True)
    return jnp.max(n0, axis=0, keepdims=True), jnp.max(n1, axis=0, keepdims=True)


def _fox_kernel(q_ref, k_ref, v_ref, ccol_ref, crow_ref, fg_ref, bias_ref, o_ref, kst_ref, sa_ref, sb_ref, m_ref, acc_ref):
    hp = pl.program_id(1)
    i = pl.program_id(2)
    nkb = k_ref.shape[1] // TK
    lane = lax.broadcasted_iota(jnp.int32, (1, LANES), 1)
    first = lane < FOX_DH

    @pl.when(i == 0)
    def _():
        kst_ref[...] = jnp.zeros_like(kst_ref)
        for j in range(nkb):
            n0, n1 = _head_sqnorm_max(k_ref[0, j * TK:(j + 1) * TK, :], first)
            cmin = jnp.min(crow_ref[0, 0, :, j * TK:(j + 1) * TK], axis=-1, keepdims=True)
            kst_ref[j:j + 1, :] = jnp.where(lane == 0, n0, jnp.where(lane == 1, n1, jnp.where(
                lane == 2, cmin[0:1], jnp.where(lane == 3, cmin[1:2], 0.0))))

    q = q_ref[0]
    zero = jnp.zeros_like(q)
    qh = (jnp.where(first, q, zero), jnp.where(first, zero, q))
    ccol = ccol_ref[0]
    cq = tuple(jnp.sum(jnp.where(lane == 2 * hp + hh, ccol, 0.0), axis=-1, keepdims=True) for hh in range(2))
    lane2 = lax.broadcasted_iota(jnp.int32, (1, 2 * LANES), 1)
    sel0 = (lane2 < FOX_DH) | (lane2 == LANES)
    ones_blk = jnp.broadcast_to((lane < 2).astype(BF16), (TK, LANES))

    def scores(j, dst):
        off = pl.multiple_of(j * TK, LANES)
        k = k_ref[0, pl.ds(off, TK), :]
        ck = crow_ref[0, 0, :, pl.ds(off, TK)]
        for hh in range(2):
            dst[hh] = _dot_nt(qh[hh], k) + cq[hh] - ck[hh:hh + 1, :]

    def consume(j, src, diag):
        off = pl.multiple_of(j * TK, LANES)
        vext = jnp.concatenate([v_ref[0, pl.ds(off, TK), :], ones_blk], axis=1)
        alphas, pvs = [], []
        for hh in range(2):
            s = src[hh]
            if diag:
                s = s + bias_ref[...]
            m_old = m_ref[hh]
            m_new = jnp.maximum(m_old, jnp.max(s, axis=-1, keepdims=True))
            m_ref[hh] = m_new
            alphas.append(jnp.exp2(m_old - m_new))
            pvs.append(_dot(jnp.exp2(s - m_new).astype(BF16), vext))
        acc_ref[...] = acc_ref[...] * jnp.where(sel0, alphas[0], alphas[1]) + jnp.where(sel0, pvs[0], pvs[1])

    m_ref[...] = jnp.full(m_ref.shape, -BIG, F32)
    acc_ref[...] = jnp.zeros(acc_ref.shape, F32)
    scores(i, sa_ref)
    scores(jnp.maximum(i - 1, 0), sb_ref)
    consume(i, sa_ref, True)

    qn = _head_sqnorm_max(q, first)
    kst = kst_ref[...]
    rowj = lax.broadcasted_iota(jnp.int32, (kst.shape[0], 1), 0)
    starts = []
    for hh in range(2):
        kn = kst[:, hh:hh + 1] * 1.01
        cmin = kst[:, 2 + hh:3 + hh]
        m_low = jnp.min(m_ref[hh], axis=0, keepdims=True)
        ub = jnp.max(cq[hh], axis=0, keepdims=True) - cmin + jnp.sqrt(qn[hh] * kn) - m_low
        skippable = (ub <= -SKIP_LOG2) & (rowj < i)
        starts.append(jnp.min(jnp.where(skippable, kst.shape[0], rowj)))
    j0 = jnp.minimum(starts[0], starts[1])
    nb = i - j0

    def blk(kk):
        return jnp.where(kk == 0, i - 1, j0 + kk - 1)

    def pair(n, carry):
        scores(blk(2 * n + 1), sa_ref)
        consume(blk(2 * n), sb_ref, False)
        scores(blk(2 * n + 2), sb_ref)
        consume(blk(2 * n + 1), sa_ref, False)
        return carry

    npairs = jnp.maximum(nb - 1, 0) // 2
    lax.fori_loop(0, npairs, pair, 0)

    @pl.when((nb > 0) & (nb % 2 == 0))
    def _():
        scores(blk(nb - 1), sa_ref)
        consume(blk(nb - 2), sb_ref, False)
        consume(blk(nb - 1), sa_ref, False)

    @pl.when(nb % 2 == 1)
    def _():
        consume(blk(nb - 1), sb_ref, False)

    acc = acc_ref[...]
    o = acc[:, 0:LANES] / jnp.where(first, acc[:, LANES:LANES + 1], acc[:, LANES + 1:LANES + 2])
    row = lax.broadcasted_iota(jnp.int32, (TQ, 1), 0)
    valid = ((i * TQ + row) >= (PAD - N_META)).astype(F32)
    o_ref[0] = (o * _silu(fg_ref[0]) * valid).astype(BF16)


def _fox(qkv3, ccol, crow4, rest3, B, L):
    nq = L // TQ
    causal_bias = jnp.where(jnp.arange(TK)[None, :] <= jnp.arange(TQ)[:, None], 0.0, -BIG).astype(F32)
    hpairs = FOX_HEADS // 2
    return pl.pallas_call(
        _fox_kernel,
        grid=(B, hpairs, nq),
        in_specs=[
            pl.BlockSpec((1, TQ, LANES), lambda b, h, i: (b, i, h)),
            pl.BlockSpec((1, L, LANES), lambda b, h, i: (b, 0, FOX_W // LANES + h)),
            pl.BlockSpec((1, L, LANES), lambda b, h, i: (b, 0, 2 * FOX_W // LANES + h)),
            pl.BlockSpec((1, TQ, LANES), lambda b, h, i: (b, i, 0)),
            pl.BlockSpec((1, 1, 2, L), lambda b, h, i: (b, h, 0, 0)),
            pl.BlockSpec((1, TQ, LANES), lambda b, h, i: (b, i, R_FG // LANES + h)),
            pl.BlockSpec((TQ, TK), lambda b, h, i: (0, 0)),
        ],
        out_specs=pl.BlockSpec((1, TQ, LANES), lambda b, h, i: (b, i, h)),
        out_shape=jax.ShapeDtypeStruct((B, L, FOX_W), BF16),
        scratch_shapes=[
            pltpu.VMEM((2 * SUBLANES, LANES), F32),
            pltpu.VMEM((2, TQ, TK), F32),
            pltpu.VMEM((2, TQ, TK), F32),
            pltpu.VMEM((2, TQ, 1), F32),
            pltpu.VMEM((TQ, 2 * LANES), F32),
        ],
        compiler_params=pltpu.CompilerParams(
            dimension_semantics=("arbitrary", "arbitrary", "arbitrary"), vmem_limit_bytes=VMEM_LIMIT),
        name="fox",
    )(qkv3, qkv3, qkv3, ccol, crow4, rest3, causal_bias)


def _outproj_math(yf, yg, yl, h, w_ref, g_ref):
    acc = _dot(yf, w_ref[0, 0:FOX_W, :])
    acc = acc + _dot(yg, w_ref[0, FOX_W:FOX_W + GLA_W, :])
    acc = acc + _dot(yl, w_ref[0, FOX_W + GLA_W:D_MIX, :])
    ms = jnp.mean(acc * acc, axis=-1, keepdims=True)
    return h + acc * lax.rsqrt(ms + EPS) * g_ref[0]


def _outproj_kernel(yf_ref, yg_ref, yl_ref, h_ref, w_ref, g_ref, o_ref):
    o_ref[...] = _outproj_math(yf_ref[...], yg_ref[...], yl_ref[...], h_ref[...], w_ref, g_ref)


def _outproj(yf, yg, yl, h2, w_all, g_all, layer, tm):
    n = h2.shape[0]
    return pl.pallas_call(
        _outproj_kernel,
        grid=(n // tm,),
        in_specs=[
            pl.BlockSpec((tm, FOX_W), lambda i: (i, 0)),
            pl.BlockSpec((tm, GLA_W), lambda i: (i, 0)),
            pl.BlockSpec((tm, LRU_W), lambda i: (i, 0)),
            pl.BlockSpec((tm, D_MODEL), lambda i: (i, 0)),
            pl.BlockSpec((1, D_MIX, D_MODEL), lambda i: (layer, 0, 0)),
            pl.BlockSpec((1, 1, D_MODEL), lambda i: (layer, 0, 0)),
        ],
        out_specs=pl.BlockSpec((tm, D_MODEL), lambda i: (i, 0)),
        out_shape=jax.ShapeDtypeStruct((n, D_MODEL), F32),
        compiler_params=pltpu.CompilerParams(
            dimension_semantics=("arbitrary",), vmem_limit_bytes=VMEM_LIMIT),
        name="outproj",
    )(yf, yg, yl, h2, w_all, g_all)


def _outproj_last_kernel(*refs):
    n = OUT_PARTS
    yf, yg, yl, h = [jnp.concatenate([r[0] for r in refs[a * n:(a + 1) * n]], axis=0) for a in range(4)]
    w_ref, g_ref, o_ref = refs[4 * n:]
    o_ref[0] = _outproj_math(yf, yg, yl, h, w_ref, g_ref)


def _outproj_last(yf3, yg3, yl3, h3, w_all, g_all, layer, S):
    B = h3.shape[0]
    tm = OUT_PARTS * PAD

    def part_specs(width):
        return [pl.BlockSpec((1, PAD, width), functools.partial(lambda r, b, j: (b, 1 + OUT_PARTS * j + r, 0), r))
                for r in range(OUT_PARTS)]

    in_specs = part_specs(FOX_W) + part_specs(GLA_W) + part_specs(LRU_W) + part_specs(D_MODEL) + [
        pl.BlockSpec((1, D_MIX, D_MODEL), lambda b, j: (layer, 0, 0)),
        pl.BlockSpec((1, 1, D_MODEL), lambda b, j: (layer, 0, 0)),
    ]
    args = [yf3] * OUT_PARTS + [yg3] * OUT_PARTS + [yl3] * OUT_PARTS + [h3] * OUT_PARTS + [w_all, g_all]
    return pl.pallas_call(
        _outproj_last_kernel,
        grid=(B, S // tm),
        in_specs=in_specs,
        out_specs=pl.BlockSpec((1, tm, D_MODEL), lambda b, j: (b, j, 0)),
        out_shape=jax.ShapeDtypeStruct((B, S, D_MODEL), F32),
        compiler_params=pltpu.CompilerParams(
            dimension_semantics=("arbitrary", "arbitrary"), vmem_limit_bytes=VMEM_LIMIT),
        name="outproj_last",
    )(*args)


def _level_constants():
    T = T_SEQ
    t = jnp.arange(T)[:, None]
    s = jnp.arange(T)[None, :]
    mall = (s <= t).astype(BF16)
    x = t ^ s
    level = jnp.zeros((T, T), jnp.int32)
    for j in range(1, N_LEVELS + 1):
        level = jnp.where((x >> (j - 1)) == 1, j, level)
    level = jnp.where(s > t, -1, level)
    return mall, jnp.tile(level, (GLA_HEADS, 1))


def _prepare_weights(w_in, w_out, b_f):
    depth = w_in.shape[0]
    sizes = (FOX_W, FOX_W, FOX_W, FOX_HEADS, FOX_W, GLA_KW, GLA_KW, GLA_W, GLA_RANK, GLA_W, LRU_W, LRU_W)
    names = ("fq", "fk", "fv", "ff", "fg", "gq", "gk", "gv", "ga", "gg", "lx", "lg")
    offs, acc = {}, 0
    for nm, sz in zip(names, sizes):
        offs[nm] = acc
        acc += sz
    head_order = jnp.argsort(b_f, axis=-1)

    def head_cols(nm, width):
        return (offs[nm] + head_order[:, :, None] * width + jnp.arange(width)[None, None, :]).reshape(depth, -1)

    n_fox = offs["gq"]
    fox_cols = jnp.concatenate([head_cols("fq", FOX_DH), head_cols("fk", FOX_DH), head_cols("fv", FOX_DH),
                                head_cols("fg", FOX_DH), head_cols("ff", 1)], axis=1)
    fox = jnp.take_along_axis(w_in[:, :, 0:n_fox], fox_cols[:, None, :], axis=2)
    fq, fkv, fg, ff = (fox[:, :, 0:FOX_W], fox[:, :, FOX_W:3 * FOX_W], fox[:, :, 3 * FOX_W:4 * FOX_W],
                       fox[:, :, 4 * FOX_W:])
    seg = lambda nm, width: w_in[:, :, offs[nm]:offs[nm] + width]
    pad = jnp.zeros(w_in.shape[:2] + (LANES - FOX_HEADS - GLA_RANK,), w_in.dtype)
    w_all = jnp.concatenate([
        fq * (FOX_DH ** -0.5 * LOG2E), fkv, fg, seg("gq", GLA_KW) * (GLA_DK ** -0.5), seg("gk", GLA_KW),
        seg("gv", GLA_W), seg("gg", GLA_W), seg("lx", LRU_W), seg("lg", LRU_W), ff, seg("ga", GLA_RANK), pad],
        axis=2).astype(BF16)
    fox_rows = (head_order[:, :, None] * FOX_DH + jnp.arange(FOX_DH)[None, None, :]).reshape(depth, -1)
    wo_all = jnp.concatenate([jnp.take_along_axis(w_out[:, 0:FOX_W], fox_rows[:, :, None], axis=1),
                              w_out[:, FOX_W:]], axis=1).astype(BF16)
    return w_all, wo_all, head_order


def kernel(x, meta, pre_g, w_in, b_f, w_a2, b_a, gla_norm_g, conv_w, conv_b, w_r, b_r, w_i, b_i, lru_lambda, w_out, post_g):
    B, S, D = x.shape
    L = S + PAD
    tm_in = 512
    tm_out = 512 if (B * L) % 512 == 0 else 256
    assert D == D_MODEL and L % T_SEQ == 0 and L % TQ == 0 and L // TK <= 2 * SUBLANES
    assert (B * L) % tm_in == 0 and S % (OUT_PARTS * PAD) == 0
    depth = w_in.shape[0]
    dt = x.dtype
    h = jnp.concatenate([jnp.zeros((B, PAD - N_META, D), dt),
                         jnp.broadcast_to(meta.astype(dt)[None], (B, N_META, D)), x], axis=1)
    h2 = h.reshape(B * L, D)

    w_all, wo_all, head_order = _prepare_weights(w_in, w_out, b_f)
    row3 = lambda a: a[:, None, :]
    wa2 = jnp.zeros((depth, LANES, GLA_KW), F32).at[:, FOX_HEADS:FOX_HEADS + GLA_RANK].set(w_a2).astype(BF16)
    bfp = jnp.zeros((depth, 1, LANES), F32).at[:, 0, 0:FOX_HEADS].set(jnp.take_along_axis(b_f, head_order, axis=1))
    wri = jnp.concatenate([w_r, w_i], axis=-1).astype(BF16)
    shared = list(_level_constants())
    stacked = [wa2, row3(b_a), bfp, row3(gla_norm_g), conv_w, row3(conv_b), wri, row3(b_r), row3(b_i),
               row3(lru_lambda)]
    pre_g3, post_g3 = row3(pre_g), row3(post_g)

    out = None
    for l in range(depth):
        qkv, rest = _inproj(h2, pre_g3, w_all, l, tm_in)
        rest3 = rest.reshape(B, L, REST_W)
        ygla, ylru, ccol, crow = _seqmix(rest3, shared, stacked, l, B, L)
        yfox = _fox(qkv.reshape(B, L, QKV_W), ccol, crow.reshape(B, FOX_HEADS // 2, 2, L), rest3, B, L)
        if l + 1 < depth:
            h2 = _outproj(yfox.reshape(B * L, FOX_W), ygla.reshape(B * L, GLA_W), ylru.reshape(B * L, LRU_W),
                          h2, wo_all, post_g3, l, tm_out)
        else:
            out = _outproj_last(yfox, ygla, ylru, h2.reshape(B, L, D), wo_all, post_g3, l, S)
    return out
```

```python
import functools

import jax
import jax.numpy as jnp
from jax import lax
from jax.experimental import pallas as pl
from jax.experimental.pallas import tpu as pltpu

F32 = jnp.float32
BF16 = jnp.bfloat16

D_MODEL = 1024
N_META = 16
PAD = 128
EPS = 1e-6
BIG = 1e30

FOX_HEADS = 8
FOX_DH = 64
FOX_W = FOX_HEADS * FOX_DH
GLA_HEADS = 4
GLA_DK = 64
GLA_DV = 128
GLA_KW = GLA_HEADS * GLA_DK
GLA_W = GLA_HEADS * GLA_DV
GLA_RANK = 16
GLA_GATE_NORM = 16.0
LRU_W = 1024
LRU_BLOCKS = 8
LRU_BS = LRU_W // LRU_BLOCKS
CONV_W = 4
LRU_C = 8.0
D_MIX = FOX_W + GLA_W + LRU_W

LANES = 128
SUBLANES = 8
VMEM_LIMIT = 56 * 1024 * 1024

QKV_W = 3 * FOX_W
R_FG = 0
R_GQ = R_FG + FOX_W
R_GK = R_GQ + GLA_KW
R_GV = R_GK + GLA_KW
R_GG = R_GV + GLA_W
R_SM = R_GG + GLA_W
REST_W = R_SM + LANES
W_G = QKV_W
W_LX = W_G + R_SM
W_LG = W_LX + LRU_W
W_SM = W_LG + LRU_W
W_ALL = W_SM + LANES
LRU_OUT_W = 3 * LRU_W

T_SEQ = 128
N_LEVELS = 7
TQ = 640
TK = 640
LOG2E = 1.4426950408889634
OUT_PARTS = 4
SKIP_LOG2 = 160.0


def _dot(a, b):
    return jnp.dot(a, b, preferred_element_type=F32)


def _dot_nt(a, b):
    return lax.dot_general(a, b, (((1,), (1,)), ((), ())), preferred_element_type=F32)


def _dot_tn(a, b):
    return lax.dot_general(a, b, (((0,), (0,)), ((), ())), preferred_element_type=F32)


def _split3(x):
    hi = x.astype(BF16)
    r = x - hi.astype(F32)
    mid = r.astype(BF16)
    lo = (r - mid.astype(F32)).astype(BF16)
    return hi, mid, lo


def _dot3(m, x):
    hi, mid, lo = _split3(x)
    return _dot(m, hi) + _dot(m, mid) + _dot(m, lo)


def _log_sigmoid(z):
    return jnp.minimum(z, 0.0) - jnp.log(1.0 + jnp.exp2(jnp.abs(z) * -LOG2E))


def _sigmoid(z):
    return 1.0 / (1.0 + jnp.exp2(z * -LOG2E))


def _silu(z):
    return z * _sigmoid(z)


def _conv_rows(x, prev, w_ref, col0):
    width = x.shape[1]
    sub = lax.broadcasted_iota(jnp.int32, (SUBLANES, 1), 0)
    groups = [prev] + [x[g * SUBLANES:(g + 1) * SUBLANES, :] for g in range(x.shape[0] // SUBLANES)]
    taps = [w_ref[0, kk:kk + 1, col0:col0 + width] for kk in range(CONV_W)]
    rolled = {}
    out = []
    for g in range(1, len(groups)):
        acc = groups[g] * taps[CONV_W - 1]
        for kk in range(1, CONV_W):
            for gi in (g - 1, g):
                if (gi, kk) not in rolled:
                    rolled[(gi, kk)] = pltpu.roll(groups[gi], kk, 0)
            acc = acc + jnp.where(sub >= kk, rolled[(g, kk)], rolled[(g - 1, kk)]) * taps[CONV_W - 1 - kk]
        out.append(acc)
    return jnp.concatenate(out, axis=0)


def _inproj_kernel(h_ref, g_ref, w_ref, cw_ref, cb_ref, wri_ref, br_ref, bi_ref, lam_ref,
                   qkv_ref, rest_ref, lru_ref, px_ref, *, n_batches, rows_per_batch):
    i = pl.program_id(0)
    tm = h_ref.shape[0]

    @pl.when(i == 0)
    def _():
        px_ref[...] = jnp.zeros_like(px_ref)

    x = h_ref[...]
    ms = jnp.mean(x * x, axis=-1, keepdims=True)
    hn = (x * lax.rsqrt(ms + EPS) * g_ref[0]).astype(BF16)
    step = 512

    def projection_stages():
        for c in range(0, QKV_W, step):
            qkv_ref[:, c:c + step] = _dot(hn, w_ref[0, :, c:c + step]).astype(BF16)
            yield
        for c in range(0, R_SM, step):
            rest_ref[:, c:c + step] = _dot(hn, w_ref[0, :, W_G + c:W_G + c + step])
            yield
        rest_ref[:, R_SM:REST_W] = _dot(hn, w_ref[0, :, W_SM:W_ALL])

    def lru_stages():
        r = i * tm + lax.broadcasted_iota(jnp.int32, (tm, 1), 0)
        invalid = (r < 0)
        for b in range(n_batches):
            invalid = invalid | ((r >= b * rows_per_batch) & (r < b * rows_per_batch + PAD - N_META))
        valid = jnp.where(invalid, 0.0, 1.0)
        soft = jnp.maximum(-lam_ref[0], 0.0) + jnp.log(1.0 + jnp.exp(-jnp.abs(lam_ref[0])))
        for c in range(0, LRU_W, step):
            lx = _dot(hn, w_ref[0, :, W_LX + c:W_LX + c + step])
            yield
            xc = _conv_rows(lx, px_ref[:, c:c + step], cw_ref, c)
            px_ref[:, c:c + step] = lx[tm - SUBLANES:tm, :]
            xc = (xc + cb_ref[0, :, c:c + step]) * valid
            xcb = xc.astype(BF16)
            yield
            for blk in range(step // LRU_BS):
                sl = slice(blk * LRU_BS, (blk + 1) * LRU_BS)
                ch = slice(c + blk * LRU_BS, c + (blk + 1) * LRU_BS)
                ri = _dot(xcb[:, sl], wri_ref[0, (c // LRU_BS) + blk])
                rg = _sigmoid(ri[:, 0:LRU_BS] + br_ref[0, :, ch])
                ig = _sigmoid(ri[:, LRU_BS:2 * LRU_BS] + bi_ref[0, :, ch])
                la = -LRU_C * rg * soft[:, ch]
                th = jnp.tanh(la)
                w = -2.0 * th / (1.0 - th)
                root = jnp.where(w > 0.0, w * lax.rsqrt(w), 0.0)
                lru_ref[:, c + blk * LRU_BS:c + (blk + 1) * LRU_BS] = jnp.exp(la)
                lru_ref[:, LRU_W + c + blk * LRU_BS:LRU_W + c + (blk + 1) * LRU_BS] = root * (ig * xc[:, sl])
                if blk % 2 == 1:
                    yield
            lg = _dot(hn, w_ref[0, :, W_LG + c:W_LG + c + step])
            lru_ref[:, 2 * LRU_W + c:2 * LRU_W + c + step] = _silu(lg)
            yield

    streams = [lru_stages(), projection_stages()]
    while streams:
        for gen in list(streams):
            if next(gen, StopIteration) is StopIteration:
                streams.remove(gen)


def _inproj(h2, g_all, w_all, lru_params, layer, tm, n_batches, rows_per_batch):
    n = h2.shape[0]
    stacked_specs = [pl.BlockSpec((1,) + c.shape[1:], functools.partial(lambda nd, i: (layer,) + (0,) * nd, c.ndim - 1))
                     for c in lru_params]
    return pl.pallas_call(
        functools.partial(_inproj_kernel, n_batches=n_batches, rows_per_batch=rows_per_batch),
        grid=(n // tm,),
        in_specs=[
            pl.BlockSpec((tm, D_MODEL), lambda i: (i, 0)),
            pl.BlockSpec((1, 1, D_MODEL), lambda i: (layer, 0, 0)),
            pl.BlockSpec((1, D_MODEL, W_ALL), lambda i: (layer, 0, 0)),
        ] + stacked_specs,
        out_specs=[
            pl.BlockSpec((tm, QKV_W), lambda i: (i, 0)),
            pl.BlockSpec((tm, REST_W), lambda i: (i, 0)),
            pl.BlockSpec((tm, LRU_OUT_W), lambda i: (i, 0)),
        ],
        out_shape=[
            jax.ShapeDtypeStruct((n, QKV_W), BF16),
            jax.ShapeDtypeStruct((n, REST_W), F32),
            jax.ShapeDtypeStruct((n, LRU_OUT_W), F32),
        ],
        scratch_shapes=[pltpu.VMEM((SUBLANES, LRU_W), F32)],
        compiler_params=pltpu.CompilerParams(
            dimension_semantics=("arbitrary",), vmem_limit_bytes=VMEM_LIMIT),
        name="inproj",
    )(h2, g_all, w_all, *lru_params)


def _level_reference(b, j, sub):
    T = b.shape[0]
    half = 1 << (j - 1)
    parts = []
    if j > 3:
        for blk in range(T >> j):
            r = (blk << j) + half - 1
            parts.append(jnp.broadcast_to(b[r:r + 1, :], (1 << j, b.shape[1])))
        return jnp.concatenate(parts, axis=0)
    for g in range(T // SUBLANES):
        x = b[g * SUBLANES:(g + 1) * SUBLANES, :]
        if j == 3:
            parts.append(jnp.broadcast_to(x[half - 1:half, :], x.shape))
            continue
        ref = x
        for off in range(-(half - 1), half + 1):
            if off != 0:
                pick = (sub & ((1 << j) - 1)) == (half - 1 + off)
                ref = jnp.where(pick, pltpu.roll(x, off % SUBLANES, 0), ref)
        parts.append(ref)
    return jnp.concatenate(parts, axis=0)


def _seqmix_kernel(gqk_ref, gv_ref, gg_ref, sm_ref, a_ref, u_ref, sl_ref,
                   mall_ref, lv_ref, wa2_ref, ba_ref, bf_ref, gng_ref,
                   ygla_ref, ylru_ref, ccol_ref, crow_ref,
                   st_ref, cc_ref, hc_ref):
    t = pl.program_id(1)
    T = T_SEQ

    @pl.when(t == 0)
    def _():
        st_ref[...] = jnp.zeros_like(st_ref)
        cc_ref[...] = jnp.zeros_like(cc_ref)
        hc_ref[...] = jnp.zeros_like(hc_ref)

    row = lax.broadcasted_iota(jnp.int32, (T, 1), 0)
    valid = ((t * T + row) >= (PAD - N_META)).astype(F32)
    tri = mall_ref[...]

    groups = T // SUBLANES
    sub = lax.broadcasted_iota(jnp.int32, (SUBLANES, 1), 0)

    def lru_stages():
        h_prev = hc_ref[...]
        for g in range(groups):
            rows = slice(g * SUBLANES, (g + 1) * SUBLANES)
            ag, ug = a_ref[0, rows, :], u_ref[0, rows, :]
            d = 1
            while d < SUBLANES:
                keep = sub >= d
                a_sh = jnp.where(keep, pltpu.roll(ag, d, 0), 1.0)
                u_sh = jnp.where(keep, pltpu.roll(ug, d, 0), 0.0)
                ug = ag * u_sh + ug
                ag = ag * a_sh
                d *= 2
            hg = ag * h_prev + ug
            h_prev = hg[SUBLANES - 1:SUBLANES, :]
            ylru_ref[0, rows, :] = (hg * sl_ref[0, rows, :] * valid[rows, :]).astype(BF16)
            if g % 2 == 1:
                yield
        hc_ref[...] = h_prev

    def gla_stages():
        sm = sm_ref[0]
        log_f = _log_sigmoid(sm + bf_ref[0]) * valid
        z = _dot(sm.astype(BF16), wa2_ref[0]) + ba_ref[0]
        log_a = _log_sigmoid(z) / GLA_GATE_NORM
        yield
        c = _dot3(tri, log_f) + cc_ref[...]
        cc_ref[...] = c[T - 1:T, :]
        c2 = c * LOG2E
        ccol_ref[0] = c2
        c_keys = jnp.where(valid > 0.0, c2, BIG)
        crow_ref[0] = c_keys.T[0:FOX_HEADS, :]
        b = _dot3(tri, log_a)
        yield
        gqk = gqk_ref[0]
        q = gqk[:, 0:GLA_KW]
        k = gqk[:, GLA_KW:2 * GLA_KW] * valid
        v = gv_ref[0].astype(BF16)
        lane_k = lax.broadcasted_iota(jnp.int32, (1, GLA_KW), 1)
        head_masks = [(lane_k >= hh * GLA_DK) & (lane_k < (hh + 1) * GLA_DK) for hh in range(GLA_HEADS)]

        def stack_heads(x):
            return jnp.concatenate([jnp.where(mk, x, 0.0).astype(BF16) for mk in head_masks], axis=0)

        lv = lv_ref[...]
        a_st = jnp.where(lv == 0, _dot_nt(stack_heads(q), k.astype(BF16)), 0.0)
        yield
        for j in range(1, N_LEVELS + 1):
            bref = _level_reference(b, j, sub)
            e = jnp.exp2(jnp.abs(b - bref) * -LOG2E)
            a_st = a_st + jnp.where(lv == j, _dot_nt(stack_heads(q * e), (k * e).astype(BF16)), 0.0)
            yield
        a_st = a_st.astype(BF16)
        b_last = b[T - 1:T, :]
        qd_st = stack_heads(q * jnp.exp(b))
        kd = (k * jnp.exp(b_last - b)).astype(BF16)
        st = st_ref[...]
        st_bf = st.astype(BF16)
        st_new = st * jnp.exp(b_last)
        gg = gg_ref[0]
        for hh in range(GLA_HEADS):
            vh = v[:, hh * GLA_DV:(hh + 1) * GLA_DV]
            o = _dot(a_st[hh * T:(hh + 1) * T], vh) + _dot_nt(qd_st[hh * T:(hh + 1) * T], st_bf)
            o = o * lax.rsqrt(jnp.mean(o * o, axis=-1, keepdims=True) + EPS)
            o = o * gng_ref[0, :, hh * GLA_DV:(hh + 1) * GLA_DV]
            y = o * _silu(gg[:, hh * GLA_DV:(hh + 1) * GLA_DV]) * valid
            ygla_ref[0, :, hh * GLA_DV:(hh + 1) * GLA_DV] = y.astype(BF16)
            st_new = st_new + jnp.where(head_masks[hh], _dot_tn(vh, kd), 0.0)
            yield
        st_ref[...] = st_new

    streams = [gla_stages(), lru_stages()]
    while streams:
        for gen in list(streams):
            if next(gen, StopIteration) is StopIteration:
                streams.remove(gen)


def _seqmix(rest3, lru3, shared, stacked, layer, B, L):
    T = T_SEQ
    nt = L // T
    consts = list(shared) + list(stacked)
    const_specs = [pl.BlockSpec(c.shape, functools.partial(lambda nd, b, t: (0,) * nd, c.ndim)) for c in shared]
    const_specs += [pl.BlockSpec((1,) + c.shape[1:], functools.partial(lambda nd, b, t: (layer,) + (0,) * nd, c.ndim - 1))
                    for c in stacked]
    return pl.pallas_call(
        _seqmix_kernel,
        grid=(B, nt),
        in_specs=[
            pl.BlockSpec((1, T, 2 * GLA_KW), lambda b, t: (b, t, R_GQ // (2 * GLA_KW))),
            pl.BlockSpec((1, T, GLA_W), lambda b, t: (b, t, R_GV // GLA_W)),
            pl.BlockSpec((1, T, GLA_W), lambda b, t: (b, t, R_GG // GLA_W)),
            pl.BlockSpec((1, T, LANES), lambda b, t: (b, t, R_SM // LANES)),
            pl.BlockSpec((1, T, LRU_W), lambda b, t: (b, t, 0)),
            pl.BlockSpec((1, T, LRU_W), lambda b, t: (b, t, 1)),
            pl.BlockSpec((1, T, LRU_W), lambda b, t: (b, t, 2)),
        ] + const_specs,
        out_specs=[
            pl.BlockSpec((1, T, GLA_W), lambda b, t: (b, t, 0)),
            pl.BlockSpec((1, T, LRU_W), lambda b, t: (b, t, 0)),
            pl.BlockSpec((1, T, LANES), lambda b, t: (b, t, 0)),
            pl.BlockSpec((1, FOX_HEADS, T), lambda b, t: (b, 0, t)),
        ],
        out_shape=[
            jax.ShapeDtypeStruct((B, L, GLA_W), BF16),
            jax.ShapeDtypeStruct((B, L, LRU_W), BF16),
            jax.ShapeDtypeStruct((B, L, LANES), F32),
            jax.ShapeDtypeStruct((B, FOX_HEADS, L), F32),
        ],
        scratch_shapes=[
            pltpu.VMEM((GLA_DV, GLA_KW), F32),
            pltpu.VMEM((1, LANES), F32),
            pltpu.VMEM((1, LRU_W), F32),
        ],
        compiler_params=pltpu.CompilerParams(
            dimension_semantics=("arbitrary", "arbitrary"), vmem_limit_bytes=VMEM_LIMIT),
        name="seqmix",
    )(rest3, rest3, rest3, rest3, lru3, lru3, lru3, *consts)


def _head_sqnorm_max(x, first):
    x2 = x.astype(F32)
    x2 = x2 * x2
    n0 = jnp.sum(jnp.where(first, x2, 0.0), axis=-1, keepdims=True)
    n1 = jnp.sum(jnp.where(first, 0.0, x2), axis=-1, keepdims=True)
    return jnp.max(n0, axis=0, keepdims=True), jnp.max(n1, axis=0, keepdims=True)


def _fox_kernel(q_ref, k_ref, v_ref, ccol_ref, crow_ref, fg_ref, bias_ref, o_ref, kst_ref, sa_ref, sb_ref, m_ref, acc_ref):
    hp = pl.program_id(1)
    i = pl.program_id(2)
    nkb = k_ref.shape[1] // TK
    lane = lax.broadcasted_iota(jnp.int32, (1, LANES), 1)
    first = lane < FOX_DH

    @pl.when(i == 0)
    def _():
        kst_ref[...] = jnp.zeros_like(kst_ref)
        for j in range(nkb):
            n0, n1 = _head_sqnorm_max(k_ref[0, j * TK:(j + 1) * TK, :], first)
            cmin = jnp.min(crow_ref[0, 0, :, j * TK:(j + 1) * TK], axis=-1, keepdims=True)
            kst_ref[j:j + 1, :] = jnp.where(lane == 0, n0, jnp.where(lane == 1, n1, jnp.where(
                lane == 2, cmin[0:1], jnp.where(lane == 3, cmin[1:2], 0.0))))

    q = q_ref[0]
    zero = jnp.zeros_like(q)
    qh = (jnp.where(first, q, zero), jnp.where(first, zero, q))
    ccol = ccol_ref[0]
    cq = tuple(jnp.sum(jnp.where(lane == 2 * hp + hh, ccol, 0.0), axis=-1, keepdims=True) for hh in range(2))
    lane2 = lax.broadcasted_iota(jnp.int32, (1, 2 * LANES), 1)
    sel0 = (lane2 < FOX_DH) | (lane2 == LANES)
    ones_blk = jnp.broadcast_to((lane < 2).astype(BF16), (TK, LANES))

    def scores(j, dst):
        off = pl.multiple_of(j * TK, LANES)
        k = k_ref[0, pl.ds(off, TK), :]
        ck = crow_ref[0, 0, :, pl.ds(off, TK)]
        for hh in range(2):
            dst[hh] = _dot_nt(qh[hh], k) + cq[hh] - ck[hh:hh + 1, :]

    def consume(j, src, diag):
        off = pl.multiple_of(j * TK, LANES)
        vext = jnp.concatenate([v_ref[0, pl.ds(off, TK), :], ones_blk], axis=1)
        alphas, pvs = [], []
        for hh in range(2):
            s = src[hh]
            if diag:
                s = s + bias_ref[...]
            m_old = m_ref[hh]
            m_new = jnp.maximum(m_old, jnp.max(s, axis=-1, keepdims=True))
            m_ref[hh] = m_new
            alphas.append(jnp.exp2(m_old - m_new))
            pvs.append(_dot(jnp.exp2(s - m_new).astype(BF16), vext))
        acc_ref[...] = acc_ref[...] * jnp.where(sel0, alphas[0], alphas[1]) + jnp.where(sel0, pvs[0], pvs[1])

    m_ref[...] = jnp.full(m_ref.shape, -BIG, F32)
    acc_ref[...] = jnp.zeros(acc_ref.shape, F32)
    scores(i, sa_ref)
    scores(jnp.maximum(i - 1, 0), sb_ref)
    consume(i, sa_ref, True)

    qn = _head_sqnorm_max(q, first)
    kst = kst_ref[...]
    rowj = lax.broadcasted_iota(jnp.int32, (kst.shape[0], 1), 0)
    starts = []
    for hh in range(2):
        kn = kst[:, hh:hh + 1] * 1.01
        cmin = kst[:, 2 + hh:3 + hh]
        m_low = jnp.min(m_ref[hh], axis=0, keepdims=True)
        ub = jnp.max(cq[hh], axis=0, keepdims=True) - cmin + jnp.sqrt(qn[hh] * kn) - m_low
        skippable = (ub <= -SKIP_LOG2) & (rowj < i)
        starts.append(jnp.min(jnp.where(skippable, kst.shape[0], rowj)))
    j0 = jnp.minimum(starts[0], starts[1])
    nb = i - j0

    def blk(kk):
        return jnp.where(kk == 0, i - 1, j0 + kk - 1)

    def pair(n, carry):
        scores(blk(2 * n + 1), sa_ref)
        consume(blk(2 * n), sb_ref, False)
        scores(blk(2 * n + 2), sb_ref)
        consume(blk(2 * n + 1), sa_ref, False)
        return carry

    npairs = jnp.maximum(nb - 1, 0) // 2
    lax.fori_loop(0, npairs, pair, 0)

    @pl.when((nb > 0) & (nb % 2 == 0))
    def _():
        scores(blk(nb - 1), sa_ref)
        consume(blk(nb - 2), sb_ref, False)
        consume(blk(nb - 1), sa_ref, False)

    @pl.when(nb % 2 == 1)
    def _():
        consume(blk(nb - 1), sb_ref, False)

    acc = acc_ref[...]
    o = acc[:, 0:LANES] / jnp.where(first, acc[:, LANES:LANES + 1], acc[:, LANES + 1:LANES + 2])
    row = lax.broadcasted_iota(jnp.int32, (TQ, 1), 0)
    valid = ((i * TQ + row) >= (PAD - N_META)).astype(F32)
    o_ref[0] = (o * _silu(fg_ref[0]) * valid).astype(BF16)


def _fox(qkv3, ccol, crow4, rest3, B, L):
    nq = L // TQ
    causal_bias = jnp.where(jnp.arange(TK)[None, :] <= jnp.arange(TQ)[:, None], 0.0, -BIG).astype(F32)
    hpairs = FOX_HEADS // 2
    return pl.pallas_call(
        _fox_kernel,
        grid=(B, hpairs, nq),
        in_specs=[
            pl.BlockSpec((1, TQ, LANES), lambda b, h, i: (b, i, h)),
            pl.BlockSpec((1, L, LANES), lambda b, h, i: (b, 0, FOX_W // LANES + h)),
            pl.BlockSpec((1, L, LANES), lambda b, h, i: (b, 0, 2 * FOX_W // LANES + h)),
            pl.BlockSpec((1, TQ, LANES), lambda b, h, i: (b, i, 0)),
            pl.BlockSpec((1, 1, 2, L), lambda b, h, i: (b, h, 0, 0)),
            pl.BlockSpec((1, TQ, LANES), lambda b, h, i: (b, i, R_FG // LANES + h)),
            pl.BlockSpec((TQ, TK), lambda b, h, i: (0, 0)),
        ],
        out_specs=pl.BlockSpec((1, TQ, LANES), lambda b, h, i: (b, i, h)),
        out_shape=jax.ShapeDtypeStruct((B, L, FOX_W), BF16),
        scratch_shapes=[
            pltpu.VMEM((2 * SUBLANES, LANES), F32),
            pltpu.VMEM((2, TQ, TK), F32),
            pltpu.VMEM((2, TQ, TK), F32),
            pltpu.VMEM((2, TQ, 1), F32),
            pltpu.VMEM((TQ, 2 * LANES), F32),
        ],
        compiler_params=pltpu.CompilerParams(
            dimension_semantics=("arbitrary", "arbitrary", "arbitrary"), vmem_limit_bytes=VMEM_LIMIT),
        name="fox",
    )(qkv3, qkv3, qkv3, ccol, crow4, rest3, causal_bias)


def _outproj_math(yf, yg, yl, h, w_ref, g_ref):
    acc = _dot(yf, w_ref[0, 0:FOX_W, :])
    acc = acc + _dot(yg, w_ref[0, FOX_W:FOX_W + GLA_W, :])
    acc = acc + _dot(yl, w_ref[0, FOX_W + GLA_W:D_MIX, :])
    ms = jnp.mean(acc * acc, axis=-1, keepdims=True)
    return h + acc * lax.rsqrt(ms + EPS) * g_ref[0]


def _outproj_kernel(yf_ref, yg_ref, yl_ref, h_ref, w_ref, g_ref, o_ref):
    o_ref[...] = _outproj_math(yf_ref[...], yg_ref[...], yl_ref[...], h_ref[...], w_ref, g_ref)


def _outproj(yf, yg, yl, h2, w_all, g_all, layer, tm):
    n = h2.shape[0]
    return pl.pallas_call(
        _outproj_kernel,
        grid=(n // tm,),
        in_specs=[
            pl.BlockSpec((tm, FOX_W), lambda i: (i, 0)),
            pl.BlockSpec((tm, GLA_W), lambda i: (i, 0)),
            pl.BlockSpec((tm, LRU_W), lambda i: (i, 0)),
            pl.BlockSpec((tm, D_MODEL), lambda i: (i, 0)),
            pl.BlockSpec((1, D_MIX, D_MODEL), lambda i: (layer, 0, 0)),
            pl.BlockSpec((1, 1, D_MODEL), lambda i: (layer, 0, 0)),
        ],
        out_specs=pl.BlockSpec((tm, D_MODEL), lambda i: (i, 0)),
        out_shape=jax.ShapeDtypeStruct((n, D_MODEL), F32),
        compiler_params=pltpu.CompilerParams(
            dimension_semantics=("arbitrary",), vmem_limit_bytes=VMEM_LIMIT),
        name="outproj",
    )(yf, yg, yl, h2, w_all, g_all)


def _outproj_last_kernel(*refs):
    n = OUT_PARTS
    yf, yg, yl, h = [jnp.concatenate([r[0] for r in refs[a * n:(a + 1) * n]], axis=0) for a in range(4)]
    w_ref, g_ref, o_ref = refs[4 * n:]
    o_ref[0] = _outproj_math(yf, yg, yl, h, w_ref, g_ref)


def _outproj_last(yf3, yg3, yl3, h3, w_all, g_all, layer, S):
    B = h3.shape[0]
    tm = OUT_PARTS * PAD

    def part_specs(width):
        return [pl.BlockSpec((1, PAD, width), functools.partial(lambda r, b, j: (b, 1 + OUT_PARTS * j + r, 0), r))
                for r in range(OUT_PARTS)]

    in_specs = part_specs(FOX_W) + part_specs(GLA_W) + part_specs(LRU_W) + part_specs(D_MODEL) + [
        pl.BlockSpec((1, D_MIX, D_MODEL), lambda b, j: (layer, 0, 0)),
        pl.BlockSpec((1, 1, D_MODEL), lambda b, j: (layer, 0, 0)),
    ]
    args = [yf3] * OUT_PARTS + [yg3] * OUT_PARTS + [yl3] * OUT_PARTS + [h3] * OUT_PARTS + [w_all, g_all]
    return pl.pallas_call(
        _outproj_last_kernel,
        grid=(B, S // tm),
        in_specs=in_specs,
        out_specs=pl.BlockSpec((1, tm, D_MODEL), lambda b, j: (b, j, 0)),
        out_shape=jax.ShapeDtypeStruct((B, S, D_MODEL), F32),
        compiler_params=pltpu.CompilerParams(
            dimension_semantics=("arbitrary", "arbitrary"), vmem_limit_bytes=VMEM_LIMIT),
        name="outproj_last",
    )(*args)


def _level_constants():
    T = T_SEQ
    t = jnp.arange(T)[:, None]
    s = jnp.arange(T)[None, :]
    mall = (s <= t).astype(BF16)
    x = t ^ s
    level = jnp.zeros((T, T), jnp.int32)
    for j in range(1, N_LEVELS + 1):
        level = jnp.where((x >> (j - 1)) == 1, j, level)
    level = jnp.where(s > t, -1, level)
    return mall, jnp.tile(level, (GLA_HEADS, 1))


def _prepare_weights(w_in, w_out, b_f):
    depth = w_in.shape[0]
    sizes = (FOX_W, FOX_W, FOX_W, FOX_HEADS, FOX_W, GLA_KW, GLA_KW, GLA_W, GLA_RANK, GLA_W, LRU_W, LRU_W)
    names = ("fq", "fk", "fv", "ff", "fg", "gq", "gk", "gv", "ga", "gg", "lx", "lg")
    offs, acc = {}, 0
    for nm, sz in zip(names, sizes):
        offs[nm] = acc
        acc += sz
    head_order = jnp.argsort(b_f, axis=-1)

    def head_cols(nm, width):
        return (offs[nm] + head_order[:, :, None] * width + jnp.arange(width)[None, None, :]).reshape(depth, -1)

    n_fox = offs["gq"]
    fox_cols = jnp.concatenate([head_cols("fq", FOX_DH), head_cols("fk", FOX_DH), head_cols("fv", FOX_DH),
                                head_cols("fg", FOX_DH), head_cols("ff", 1)], axis=1)
    fox = jnp.take_along_axis(w_in[:, :, 0:n_fox], fox_cols[:, None, :], axis=2)
    fq, fkv, fg, ff = (fox[:, :, 0:FOX_W], fox[:, :, FOX_W:3 * FOX_W], fox[:, :, 3 * FOX_W:4 * FOX_W],
                       fox[:, :, 4 * FOX_W:])
    seg = lambda nm, width: w_in[:, :, offs[nm]:offs[nm] + width]
    pad = jnp.zeros(w_in.shape[:2] + (LANES - FOX_HEADS - GLA_RANK,), w_in.dtype)
    w_all = jnp.concatenate([
        fq * (FOX_DH ** -0.5 * LOG2E), fkv, fg, seg("gq", GLA_KW) * (GLA_DK ** -0.5), seg("gk", GLA_KW),
        seg("gv", GLA_W), seg("gg", GLA_W), seg("lx", LRU_W), seg("lg", LRU_W), ff, seg("ga", GLA_RANK), pad],
        axis=2).astype(BF16)
    fox_rows = (head_order[:, :, None] * FOX_DH + jnp.arange(FOX_DH)[None, None, :]).reshape(depth, -1)
    wo_all = jnp.concatenate([jnp.take_along_axis(w_out[:, 0:FOX_W], fox_rows[:, :, None], axis=1),
                              w_out[:, FOX_W:]], axis=1).astype(BF16)
    return w_all, wo_all, head_order


def kernel(x, meta, pre_g, w_in, b_f, w_a2, b_a, gla_norm_g, conv_w, conv_b, w_r, b_r, w_i, b_i, lru_lambda, w_out, post_g):
    B, S, D = x.shape
    L = S + PAD
    tm_in = 512
    tm_out = 512 if (B * L) % 512 == 0 else 256
    assert D == D_MODEL and L % T_SEQ == 0 and L % TQ == 0 and L // TK <= 2 * SUBLANES
    assert (B * L) % tm_in == 0 and S % (OUT_PARTS * PAD) == 0
    depth = w_in.shape[0]
    dt = x.dtype
    h = jnp.concatenate([jnp.zeros((B, PAD - N_META, D), dt),
                         jnp.broadcast_to(meta.astype(dt)[None], (B, N_META, D)), x], axis=1)
    h2 = h.reshape(B * L, D)

    w_all, wo_all, head_order = _prepare_weights(w_in, w_out, b_f)
    row3 = lambda a: a[:, None, :]
    wa2 = jnp.zeros((depth, LANES, GLA_KW), F32).at[:, FOX_HEADS:FOX_HEADS + GLA_RANK].set(w_a2).astype(BF16)
    bfp = jnp.zeros((depth, 1, LANES), F32).at[:, 0, 0:FOX_HEADS].set(jnp.take_along_axis(b_f, head_order, axis=1))
    wri = jnp.concatenate([w_r, w_i], axis=-1).astype(BF16)
    shared = list(_level_constants())
    stacked = [wa2, row3(b_a), bfp, row3(gla_norm_g)]
    lru_params = [conv_w, row3(conv_b), wri, row3(b_r), row3(b_i), row3(lru_lambda)]
    pre_g3, post_g3 = row3(pre_g), row3(post_g)

    out = None
    for l in range(depth):
        qkv, rest, lru = _inproj(h2, pre_g3, w_all, lru_params, l, tm_in, B, L)
        rest3 = rest.reshape(B, L, REST_W)
        ygla, ylru, ccol, crow = _seqmix(rest3, lru.reshape(B, L, LRU_OUT_W), shared, stacked, l, B, L)
        yfox = _fox(qkv.reshape(B, L, QKV_W), ccol, crow.reshape(B, FOX_HEADS // 2, 2, L), rest3, B, L)
        if l + 1 < depth:
            h2 = _outproj(yfox.reshape(B * L, FOX_W), ygla.reshape(B * L, GLA_W), ylru.reshape(B * L, LRU_W),
                          h2, wo_all, post_g3, l, tm_out)
        else:
            out = _outproj_last(yfox, ygla, ylru, h2.reshape(B, L, D), wo_all, post_g3, l, S)
    return out
```

```python
import functools

import jax
import jax.numpy as jnp
from jax import lax
from jax.experimental import pallas as pl
from jax.experimental.pallas import tpu as pltpu

F32 = jnp.float32
BF16 = jnp.bfloat16

D_MODEL = 1024
N_META = 16
PAD = 128
EPS = 1e-6
BIG = 1e30

FOX_HEADS = 8
FOX_DH = 64
FOX_W = FOX_HEADS * FOX_DH
GLA_HEADS = 4
GLA_DK = 64
GLA_DV = 128
GLA_KW = GLA_HEADS * GLA_DK
GLA_W = GLA_HEADS * GLA_DV
GLA_RANK = 16
GLA_GATE_NORM = 16.0
LRU_W = 1024
LRU_BLOCKS = 8
LRU_BS = LRU_W // LRU_BLOCKS
CONV_W = 4
LRU_C = 8.0
D_MIX = FOX_W + GLA_W + LRU_W

LANES = 128
SUBLANES = 8
VMEM_LIMIT = 56 * 1024 * 1024

QKV_W = 3 * FOX_W
R_FG = 0
R_GQ = R_FG + FOX_W
R_GK = R_GQ + GLA_KW
R_GV = R_GK + GLA_KW
R_GG = R_GV + GLA_W
R_SM = R_GG + GLA_W
REST_W = R_SM + LANES
W_G = QKV_W
W_LX = W_G + R_SM
W_LG = W_LX + LRU_W
W_SM = W_LG + LRU_W
W_ALL = W_SM + LANES
LRU_OUT_W = 3 * LRU_W

T_SEQ = 128
N_LEVELS = 7
TQ = 640
TK = 640
LOG2E = 1.4426950408889634
OUT_PARTS = 4
SKIP_LOG2 = 160.0


def _dot(a, b):
    return jnp.dot(a, b, preferred_element_type=F32)


def _dot_nt(a, b):
    return lax.dot_general(a, b, (((1,), (1,)), ((), ())), preferred_element_type=F32)


def _dot_tn(a, b):
    return lax.dot_general(a, b, (((0,), (0,)), ((), ())), preferred_element_type=F32)


def _split3(x):
    hi = x.astype(BF16)
    r = x - hi.astype(F32)
    mid = r.astype(BF16)
    lo = (r - mid.astype(F32)).astype(BF16)
    return hi, mid, lo


def _dot3(m, x):
    hi, mid, lo = _split3(x)
    return _dot(m, hi) + _dot(m, mid) + _dot(m, lo)


def _log_sigmoid(z):
    return jnp.minimum(z, 0.0) - jnp.log(1.0 + jnp.exp2(jnp.abs(z) * -LOG2E))


def _sigmoid(z):
    return 1.0 / (1.0 + jnp.exp2(z * -LOG2E))


def _silu(z):
    return z * _sigmoid(z)


def _conv_rows(x, prev, w_ref, col0):
    width = x.shape[1]
    sub = lax.broadcasted_iota(jnp.int32, (SUBLANES, 1), 0)
    groups = [prev] + [x[g * SUBLANES:(g + 1) * SUBLANES, :] for g in range(x.shape[0] // SUBLANES)]
    taps = [w_ref[0, kk:kk + 1, col0:col0 + width] for kk in range(CONV_W)]
    rolled = {}
    out = []
    for g in range(1, len(groups)):
        acc = groups[g] * taps[CONV_W - 1]
        for kk in range(1, CONV_W):
            for gi in (g - 1, g):
                if (gi, kk) not in rolled:
                    rolled[(gi, kk)] = pltpu.roll(groups[gi], kk, 0)
            acc = acc + jnp.where(sub >= kk, rolled[(g, kk)], rolled[(g - 1, kk)]) * taps[CONV_W - 1 - kk]
        out.append(acc)
    return jnp.concatenate(out, axis=0)


def _inproj_kernel(h_ref, g_ref, w_ref, cw_ref, cb_ref, wri_ref, br_ref, bi_ref, lam_ref,
                   qkv_ref, rest_ref, lru_ref, px_ref, *, n_batches, rows_per_batch):
    i = pl.program_id(0)
    tm = h_ref.shape[0]

    @pl.when(i == 0)
    def _():
        px_ref[...] = jnp.zeros_like(px_ref)

    x = h_ref[...]
    ms = jnp.mean(x * x, axis=-1, keepdims=True)
    hn = (x * lax.rsqrt(ms + EPS) * g_ref[0]).astype(BF16)
    step = 512

    def projection_stages():
        for c in range(0, QKV_W, step):
            qkv_ref[:, c:c + step] = _dot(hn, w_ref[0, :, c:c + step]).astype(BF16)
            yield
        for c in range(0, R_SM, step):
            rest_ref[:, c:c + step] = _dot(hn, w_ref[0, :, W_G + c:W_G + c + step])
            yield
        rest_ref[:, R_SM:REST_W] = _dot(hn, w_ref[0, :, W_SM:W_ALL])

    def lru_stages():
        r = i * tm + lax.broadcasted_iota(jnp.int32, (tm, 1), 0)
        invalid = (r < 0)
        for b in range(n_batches):
            invalid = invalid | ((r >= b * rows_per_batch) & (r < b * rows_per_batch + PAD - N_META))
        valid = jnp.where(invalid, 0.0, 1.0)
        soft = jnp.maximum(-lam_ref[0], 0.0) + jnp.log(1.0 + jnp.exp(-jnp.abs(lam_ref[0])))
        for c in range(0, LRU_W, step):
            lx = _dot(hn, w_ref[0, :, W_LX + c:W_LX + c + step])
            yield
            xc = _conv_rows(lx, px_ref[:, c:c + step], cw_ref, c)
            px_ref[:, c:c + step] = lx[tm - SUBLANES:tm, :]
            xc = (xc + cb_ref[0, :, c:c + step]) * valid
            xcb = xc.astype(BF16)
            yield
            for blk in range(step // LRU_BS):
                sl = slice(blk * LRU_BS, (blk + 1) * LRU_BS)
                ch = slice(c + blk * LRU_BS, c + (blk + 1) * LRU_BS)
                ri = _dot(xcb[:, sl], wri_ref[0, (c // LRU_BS) + blk])
                rg = _sigmoid(ri[:, 0:LRU_BS] + br_ref[0, :, ch])
                ig = _sigmoid(ri[:, LRU_BS:2 * LRU_BS] + bi_ref[0, :, ch])
                la = -LRU_C * rg * soft[:, ch]
                th = jnp.tanh(la)
                w = -2.0 * th / (1.0 - th)
                root = jnp.where(w > 0.0, w * lax.rsqrt(w), 0.0)
                lru_ref[:, c + blk * LRU_BS:c + (blk + 1) * LRU_BS] = jnp.exp(la)
                lru_ref[:, LRU_W + c + blk * LRU_BS:LRU_W + c + (blk + 1) * LRU_BS] = root * (ig * xc[:, sl])
                if blk % 2 == 1:
                    yield
            lg = _dot(hn, w_ref[0, :, W_LG + c:W_LG + c + step])
            lru_ref[:, 2 * LRU_W + c:2 * LRU_W + c + step] = _silu(lg)
            yield

    streams = [lru_stages(), projection_stages()]
    while streams:
        for gen in list(streams):
            if next(gen, StopIteration) is StopIteration:
                streams.remove(gen)


def _inproj(h2, g_all, w_all, lru_params, layer, tm, n_batches, rows_per_batch):
    n = h2.shape[0]
    stacked_specs = [pl.BlockSpec((1,) + c.shape[1:], functools.partial(lambda nd, i: (layer,) + (0,) * nd, c.ndim - 1))
                     for c in lru_params]
    return pl.pallas_call(
        functools.partial(_inproj_kernel, n_batches=n_batches, rows_per_batch=rows_per_batch),
        grid=(n // tm,),
        in_specs=[
            pl.BlockSpec((tm, D_MODEL), lambda i: (i, 0)),
            pl.BlockSpec((1, 1, D_MODEL), lambda i: (layer, 0, 0)),
            pl.BlockSpec((1, D_MODEL, W_ALL), lambda i: (layer, 0, 0)),
        ] + stacked_specs,
        out_specs=[
            pl.BlockSpec((tm, QKV_W), lambda i: (i, 0)),
            pl.BlockSpec((tm, REST_W), lambda i: (i, 0)),
            pl.BlockSpec((tm, LRU_OUT_W), lambda i: (i, 0)),
        ],
        out_shape=[
            jax.ShapeDtypeStruct((n, QKV_W), BF16),
            jax.ShapeDtypeStruct((n, REST_W), F32),
            jax.ShapeDtypeStruct((n, LRU_OUT_W), F32),
        ],
        scratch_shapes=[pltpu.VMEM((SUBLANES, LRU_W), F32)],
        compiler_params=pltpu.CompilerParams(
            dimension_semantics=("arbitrary",), vmem_limit_bytes=VMEM_LIMIT),
        name="inproj",
    )(h2, g_all, w_all, *lru_params)


def _level_reference(b, j, sub):
    T = b.shape[0]
    half = 1 << (j - 1)
    parts = []
    if j > 3:
        for blk in range(T >> j):
            r = (blk << j) + half - 1
            parts.append(jnp.broadcast_to(b[r:r + 1, :], (1 << j, b.shape[1])))
        return jnp.concatenate(parts, axis=0)
    for g in range(T // SUBLANES):
        x = b[g * SUBLANES:(g + 1) * SUBLANES, :]
        if j == 3:
            parts.append(jnp.broadcast_to(x[half - 1:half, :], x.shape))
            continue
        ref = x
        for off in range(-(half - 1), half + 1):
            if off != 0:
                pick = (sub & ((1 << j) - 1)) == (half - 1 + off)
                ref = jnp.where(pick, pltpu.roll(x, off % SUBLANES, 0), ref)
        parts.append(ref)
    return jnp.concatenate(parts, axis=0)


def _seqmix_kernel(gqk_ref, gv_ref, gg_ref, sm_ref, a_ref, u_ref, sl_ref,
                   mall_ref, lv_ref, wa2_ref, ba_ref, bf_ref, gng_ref,
                   ygla_ref, ylru_ref, ccol_ref, crow_ref,
                   st_ref, cc_ref, hc_ref):
    t = pl.program_id(1)
    T = T_SEQ

    @pl.when(t == 0)
    def _():
        st_ref[...] = jnp.zeros_like(st_ref)
        cc_ref[...] = jnp.zeros_like(cc_ref)
        hc_ref[...] = jnp.zeros_like(hc_ref)

    row = lax.broadcasted_iota(jnp.int32, (T, 1), 0)
    valid = ((t * T + row) >= (PAD - N_META)).astype(F32)
    tri = mall_ref[...]

    groups = T // SUBLANES
    sub = lax.broadcasted_iota(jnp.int32, (SUBLANES, 1), 0)

    def lru_stages():
        h_prev = hc_ref[...]
        for g in range(groups):
            rows = slice(g * SUBLANES, (g + 1) * SUBLANES)
            ag, ug = a_ref[0, rows, :], u_ref[0, rows, :]
            d = 1
            while d < SUBLANES:
                keep = sub >= d
                a_sh = jnp.where(keep, pltpu.roll(ag, d, 0), 1.0)
                u_sh = jnp.where(keep, pltpu.roll(ug, d, 0), 0.0)
                ug = ag * u_sh + ug
                ag = ag * a_sh
                d *= 2
            hg = ag * h_prev + ug
            h_prev = hg[SUBLANES - 1:SUBLANES, :]
            ylru_ref[0, rows, :] = (hg * sl_ref[0, rows, :] * valid[rows, :]).astype(BF16)
            if g % 2 == 1:
                yield
        hc_ref[...] = h_prev

    def gla_stages():
        sm = sm_ref[0]
        log_f = _log_sigmoid(sm + bf_ref[0]) * valid
        z = _dot(sm.astype(BF16), wa2_ref[0]) + ba_ref[0]
        log_a = _log_sigmoid(z) / GLA_GATE_NORM
        yield
        c = _dot3(tri, log_f) + cc_ref[...]
        cc_ref[...] = c[T - 1:T, :]
        c2 = c * LOG2E
        ccol_ref[0] = c2
        c_keys = jnp.where(valid > 0.0, c2, BIG)
        crow_ref[0] = c_keys.T[0:FOX_HEADS, :]
        b = _dot3(tri, log_a)
        yield
        gqk = gqk_ref[0]
        q = gqk[:, 0:GLA_KW]
        k = gqk[:, GLA_KW:2 * GLA_KW] * valid
        v = gv_ref[0].astype(BF16)
        lane_k = lax.broadcasted_iota(jnp.int32, (1, GLA_KW), 1)
        head_masks = [(lane_k >= hh * GLA_DK) & (lane_k < (hh + 1) * GLA_DK) for hh in range(GLA_HEADS)]

        def stack_heads(x):
            return jnp.concatenate([jnp.where(mk, x, 0.0).astype(BF16) for mk in head_masks], axis=0)

        lv = lv_ref[...]
        a_st = jnp.where(lv == 0, _dot_nt(stack_heads(q), k.astype(BF16)), 0.0)
        yield
        for j in range(1, N_LEVELS + 1):
            bref = _level_reference(b, j, sub)
            e = jnp.exp2(jnp.abs(b - bref) * -LOG2E)
            a_st = a_st + jnp.where(lv == j, _dot_nt(stack_heads(q * e), (k * e).astype(BF16)), 0.0)
            yield
        a_st = a_st.astype(BF16)
        b_last = b[T - 1:T, :]
        qd_st = stack_heads(q * jnp.exp(b))
        kd = (k * jnp.exp(b_last - b)).astype(BF16)
        st = st_ref[...]
        st_bf = st.astype(BF16)
        st_new = st * jnp.exp(b_last)
        gg = gg_ref[0]
        for hh in range(GLA_HEADS):
            vh = v[:, hh * GLA_DV:(hh + 1) * GLA_DV]
            o = _dot(a_st[hh * T:(hh + 1) * T], vh) + _dot_nt(qd_st[hh * T:(hh + 1) * T], st_bf)
            o = o * lax.rsqrt(jnp.mean(o * o, axis=-1, keepdims=True) + EPS)
            o = o * gng_ref[0, :, hh * GLA_DV:(hh + 1) * GLA_DV]
            y = o * _silu(gg[:, hh * GLA_DV:(hh + 1) * GLA_DV]) * valid
            ygla_ref[0, :, hh * GLA_DV:(hh + 1) * GLA_DV] = y.astype(BF16)
            st_new = st_new + jnp.where(head_masks[hh], _dot_tn(vh, kd), 0.0)
            yield
        st_ref[...] = st_new

    streams = [gla_stages(), lru_stages()]
    while streams:
        for gen in list(streams):
            if next(gen, StopIteration) is StopIteration:
                streams.remove(gen)


def _seqmix(rest3, lru3, shared, stacked, layer, B, L):
    T = T_SEQ
    nt = L // T
    consts = list(shared) + list(stacked)
    const_specs = [pl.BlockSpec(c.shape, functools.partial(lambda nd, b, t: (0,) * nd, c.ndim)) for c in shared]
    const_specs += [pl.BlockSpec((1,) + c.shape[1:], functools.partial(lambda nd, b, t: (layer,) + (0,) * nd, c.ndim - 1))
                    for c in stacked]
    return pl.pallas_call(
        _seqmix_kernel,
        grid=(B, nt),
        in_specs=[
            pl.BlockSpec((1, T, 2 * GLA_KW), lambda b, t: (b, t, R_GQ // (2 * GLA_KW))),
            pl.BlockSpec((1, T, GLA_W), lambda b, t: (b, t, R_GV // GLA_W)),
            pl.BlockSpec((1, T, GLA_W), lambda b, t: (b, t, R_GG // GLA_W)),
            pl.BlockSpec((1, T, LANES), lambda b, t: (b, t, R_SM // LANES)),
            pl.BlockSpec((1, T, LRU_W), lambda b, t: (b, t, 0)),
            pl.BlockSpec((1, T, LRU_W), lambda b, t: (b, t, 1)),
            pl.BlockSpec((1, T, LRU_W), lambda b, t: (b, t, 2)),
        ] + const_specs,
        out_specs=[
            pl.BlockSpec((1, T, GLA_W), lambda b, t: (b, t, 0)),
            pl.BlockSpec((1, T, LRU_W), lambda b, t: (b, t, 0)),
            pl.BlockSpec((1, T, LANES), lambda b, t: (b, t, 0)),
            pl.BlockSpec((1, FOX_HEADS, T), lambda b, t: (b, 0, t)),
        ],
        out_shape=[
            jax.ShapeDtypeStruct((B, L, GLA_W), BF16),
            jax.ShapeDtypeStruct((B, L, LRU_W), BF16),
            jax.ShapeDtypeStruct((B, L, LANES), F32),
            jax.ShapeDtypeStruct((B, FOX_HEADS, L), F32),
        ],
        scratch_shapes=[
            pltpu.VMEM((GLA_DV, GLA_KW), F32),
            pltpu.VMEM((1, LANES), F32),
            pltpu.VMEM((1, LRU_W), F32),
        ],
        compiler_params=pltpu.CompilerParams(
            dimension_semantics=("arbitrary", "arbitrary"), vmem_limit_bytes=VMEM_LIMIT),
        name="seqmix",
    )(rest3, rest3, rest3, rest3, lru3, lru3, lru3, *consts)


def _head_sqnorm_max(x, first):
    x2 = x.astype(F32)
    x2 = x2 * x2
    n0 = jnp.sum(jnp.where(first, x2, 0.0), axis=-1, keepdims=True)
    n1 = jnp.sum(jnp.where(first, 0.0, x2), axis=-1, keepdims=True)
    return jnp.max(n0, axis=0, keepdims=True), jnp.max(n1, axis=0, keepdims=True)


def _fox_kernel(q_ref, k_ref, v_ref, ccol_ref, crow_ref, fg_ref, bias_ref, o_ref, kst_ref, sa_ref, sb_ref, m_ref, acc_ref):
    hp = pl.program_id(1)
    i = pl.program_id(2)
    nkb = k_ref.shape[1] // TK
    lane = lax.broadcasted_iota(jnp.int32, (1, LANES), 1)
    first = lane < FOX_DH

    @pl.when(i == 0)
    def _():
        kst_ref[...] = jnp.zeros_like(kst_ref)
        for j in range(nkb):
            n0, n1 = _head_sqnorm_max(k_ref[0, j * TK:(j + 1) * TK, :], first)
            cmin = jnp.min(crow_ref[0, 0, :, j * TK:(j + 1) * TK], axis=-1, keepdims=True)
            kst_ref[j:j + 1, :] = jnp.where(lane == 0, n0, jnp.where(lane == 1, n1, jnp.where(
                lane == 2, cmin[0:1], jnp.where(lane == 3, cmin[1:2], 0.0))))

    q = q_ref[0]
    zero = jnp.zeros_like(q)
    qh = (jnp.where(first, q, zero), jnp.where(first, zero, q))
    ccol = ccol_ref[0]
    cq = tuple(jnp.sum(jnp.where(lane == 2 * hp + hh, ccol, 0.0), axis=-1, keepdims=True) for hh in range(2))
    lane2 = lax.broadcasted_iota(jnp.int32, (1, 2 * LANES), 1)
    sel0 = (lane2 < FOX_DH) | (lane2 == LANES)
    ones_blk = jnp.broadcast_to((lane < 2).astype(BF16), (TK, LANES))

    def scores(j, dst):
        off = pl.multiple_of(j * TK, LANES)
        k = k_ref[0, pl.ds(off, TK), :]
        ck = crow_ref[0, 0, :, pl.ds(off, TK)]
        for hh in range(2):
            dst[hh] = _dot_nt(qh[hh], k) + cq[hh] - ck[hh:hh + 1, :]

    def consume(j, src, diag):
        off = pl.multiple_of(j * TK, LANES)
        vext = jnp.concatenate([v_ref[0, pl.ds(off, TK), :], ones_blk], axis=1)
        alphas, pvs = [], []
        for hh in range(2):
            s = src[hh]
            if diag:
                s = s + bias_ref[...]
            m_old = m_ref[hh]
            m_new = jnp.maximum(m_old, jnp.max(s, axis=-1, keepdims=True))
            m_ref[hh] = m_new
            alphas.append(jnp.exp2(m_old - m_new))
            pvs.append(_dot(jnp.exp2(s - m_new).astype(BF16), vext))
        acc_ref[...] = acc_ref[...] * jnp.where(sel0, alphas[0], alphas[1]) + jnp.where(sel0, pvs[0], pvs[1])

    def scores_both(j, dst):
        off = pl.multiple_of(j * TK, LANES)
        k = k_ref[0, pl.ds(off, TK), :]
        kz = jnp.zeros_like(k)
        k2 = jnp.concatenate([jnp.where(first, k, kz), jnp.where(first, kz, k)], axis=0)
        ck = crow_ref[0, 0, :, pl.ds(off, TK)]
        s2 = _dot_nt(q, k2)
        for hh in range(2):
            dst[hh] = s2[:, hh * TK:(hh + 1) * TK] + cq[hh] - ck[hh:hh + 1, :]

    def consume_both(j, src):
        off = pl.multiple_of(j * TK, LANES)
        v = v_ref[0, pl.ds(off, TK), :]
        vz = jnp.zeros_like(v)
        v2 = jnp.concatenate([
            jnp.concatenate([jnp.where(first, v, vz), jnp.broadcast_to((lane == 0).astype(BF16), (TK, LANES))], axis=1),
            jnp.concatenate([jnp.where(first, vz, v), jnp.broadcast_to((lane == 1).astype(BF16), (TK, LANES))], axis=1),
        ], axis=0)
        alphas, ps = [], []
        for hh in range(2):
            s = src[hh]
            m_old = m_ref[hh]
            m_new = jnp.maximum(m_old, jnp.max(s, axis=-1, keepdims=True))
            m_ref[hh] = m_new
            alphas.append(jnp.exp2(m_old - m_new))
            ps.append(jnp.exp2(s - m_new).astype(BF16))
        pv = _dot(jnp.concatenate(ps, axis=1), v2)
        acc_ref[...] = acc_ref[...] * jnp.where(sel0, alphas[0], alphas[1]) + pv

    m_ref[...] = jnp.full(m_ref.shape, -BIG, F32)
    acc_ref[...] = jnp.zeros(acc_ref.shape, F32)
    scores(i, sa_ref)
    scores(jnp.maximum(i - 1, 0), sb_ref)
    consume(i, sa_ref, True)

    qn = _head_sqnorm_max(q, first)
    kst = kst_ref[...]
    rowj = lax.broadcasted_iota(jnp.int32, (kst.shape[0], 1), 0)
    starts = []
    for hh in range(2):
        kn = kst[:, hh:hh + 1] * 1.01
        cmin = kst[:, 2 + hh:3 + hh]
        m_low = jnp.min(m_ref[hh], axis=0, keepdims=True)
        ub = jnp.max(cq[hh], axis=0, keepdims=True) - cmin + jnp.sqrt(qn[hh] * kn) - m_low
        skippable = (ub <= -SKIP_LOG2) & (rowj < i)
        starts.append(jnp.min(jnp.where(skippable, kst.shape[0], rowj)))
    j0 = jnp.minimum(starts[0], starts[1])
    nb = i - j0

    def blk(kk):
        return jnp.where(kk == 0, i - 1, j0 + kk - 1)

    def pair(n, carry):
        scores_both(blk(2 * n + 1), sa_ref)
        consume_both(blk(2 * n), sb_ref)
        scores_both(blk(2 * n + 2), sb_ref)
        consume_both(blk(2 * n + 1), sa_ref)
        return carry

    npairs = jnp.maximum(nb - 1, 0) // 2
    lax.fori_loop(0, npairs, pair, 0)

    @pl.when((nb > 0) & (nb % 2 == 0))
    def _():
        scores_both(blk(nb - 1), sa_ref)
        consume_both(blk(nb - 2), sb_ref)
        consume_both(blk(nb - 1), sa_ref)

    @pl.when(nb % 2 == 1)
    def _():
        consume_both(blk(nb - 1), sb_ref)

    acc = acc_ref[...]
    o = acc[:, 0:LANES] / jnp.where(first, acc[:, LANES:LANES + 1], acc[:, LANES + 1:LANES + 2])
    row = lax.broadcasted_iota(jnp.int32, (TQ, 1), 0)
    valid = ((i * TQ + row) >= (PAD - N_META)).astype(F32)
    o_ref[0] = (o * _silu(fg_ref[0]) * valid).astype(BF16)


def _fox(qkv3, ccol, crow4, rest3, B, L):
    nq = L // TQ
    causal_bias = jnp.where(jnp.arange(TK)[None, :] <= jnp.arange(TQ)[:, None], 0.0, -BIG).astype(F32)
    hpairs = FOX_HEADS // 2
    return pl.pallas_call(
        _fox_kernel,
        grid=(B, hpairs, nq),
        in_specs=[
            pl.BlockSpec((1, TQ, LANES), lambda b, h, i: (b, i, h)),
            pl.BlockSpec((1, L, LANES), lambda b, h, i: (b, 0, FOX_W // LANES + h)),
            pl.BlockSpec((1, L, LANES), lambda b, h, i: (b, 0, 2 * FOX_W // LANES + h)),
            pl.BlockSpec((1, TQ, LANES), lambda b, h, i: (b, i, 0)),
            pl.BlockSpec((1, 1, 2, L), lambda b, h, i: (b, h, 0, 0)),
            pl.BlockSpec((1, TQ, LANES), lambda b, h, i: (b, i, R_FG // LANES + h)),
            pl.BlockSpec((TQ, TK), lambda b, h, i: (0, 0)),
        ],
        out_specs=pl.BlockSpec((1, TQ, LANES), lambda b, h, i: (b, i, h)),
        out_shape=jax.ShapeDtypeStruct((B, L, FOX_W), BF16),
        scratch_shapes=[
            pltpu.VMEM((2 * SUBLANES, LANES), F32),
            pltpu.VMEM((2, TQ, TK), F32),
            pltpu.VMEM((2, TQ, TK), F32),
            pltpu.VMEM((2, TQ, 1), F32),
            pltpu.VMEM((TQ, 2 * LANES), F32),
        ],
        compiler_params=pltpu.CompilerParams(
            dimension_semantics=("arbitrary", "arbitrary", "arbitrary"), vmem_limit_bytes=VMEM_LIMIT),
        name="fox",
    )(qkv3, qkv3, qkv3, ccol, crow4, rest3, causal_bias)


def _outproj_math(yf, yg, yl, h, w_ref, g_ref):
    acc = _dot(yf, w_ref[0, 0:FOX_W, :])
    acc = acc + _dot(yg, w_ref[0, FOX_W:FOX_W + GLA_W, :])
    acc = acc + _dot(yl, w_ref[0, FOX_W + GLA_W:D_MIX, :])
    ms = jnp.mean(acc * acc, axis=-1, keepdims=True)
    return h + acc * lax.rsqrt(ms + EPS) * g_ref[0]


def _outproj_kernel(yf_ref, yg_ref, yl_ref, h_ref, w_ref, g_ref, o_ref):
    o_ref[...] = _outproj_math(yf_ref[...], yg_ref[...], yl_ref[...], h_ref[...], w_ref, g_ref)


def _outproj(yf, yg, yl, h2, w_all, g_all, layer, tm):
    n = h2.shape[0]
    return pl.pallas_call(
        _outproj_kernel,
        grid=(n // tm,),
        in_specs=[
            pl.BlockSpec((tm, FOX_W), lambda i: (i, 0)),
            pl.BlockSpec((tm, GLA_W), lambda i: (i, 0)),
            pl.BlockSpec((tm, LRU_W), lambda i: (i, 0)),
            pl.BlockSpec((tm, D_MODEL), lambda i: (i, 0)),
            pl.BlockSpec((1, D_MIX, D_MODEL), lambda i: (layer, 0, 0)),
            pl.BlockSpec((1, 1, D_MODEL), lambda i: (layer, 0, 0)),
        ],
        out_specs=pl.BlockSpec((tm, D_MODEL), lambda i: (i, 0)),
        out_shape=jax.ShapeDtypeStruct((n, D_MODEL), F32),
        compiler_params=pltpu.CompilerParams(
            dimension_semantics=("arbitrary",), vmem_limit_bytes=VMEM_LIMIT),
        name="outproj",
    )(yf, yg, yl, h2, w_all, g_all)


def _outproj_last_kernel(*refs):
    n = OUT_PARTS
    yf, yg, yl, h = [jnp.concatenate([r[0] for r in refs[a * n:(a + 1) * n]], axis=0) for a in range(4)]
    w_ref, g_ref, o_ref = refs[4 * n:]
    o_ref[0] = _outproj_math(yf, yg, yl, h, w_ref, g_ref)


def _outproj_last(yf3, yg3, yl3, h3, w_all, g_all, layer, S):
    B = h3.shape[0]
    tm = OUT_PARTS * PAD

    def part_specs(width):
        return [pl.BlockSpec((1, PAD, width), functools.partial(lambda r, b, j: (b, 1 + OUT_PARTS * j + r, 0), r))
                for r in range(OUT_PARTS)]

    in_specs = part_specs(FOX_W) + part_specs(GLA_W) + part_specs(LRU_W) + part_specs(D_MODEL) + [
        pl.BlockSpec((1, D_MIX, D_MODEL), lambda b, j: (layer, 0, 0)),
        pl.BlockSpec((1, 1, D_MODEL), lambda b, j: (layer, 0, 0)),
    ]
    args = [yf3] * OUT_PARTS + [yg3] * OUT_PARTS + [yl3] * OUT_PARTS + [h3] * OUT_PARTS + [w_all, g_all]
    return pl.pallas_call(
        _outproj_last_kernel,
        grid=(B, S // tm),
        in_specs=in_specs,
        out_specs=pl.BlockSpec((1, tm, D_MODEL), lambda b, j: (b, j, 0)),
        out_shape=jax.ShapeDtypeStruct((B, S, D_MODEL), F32),
        compiler_params=pltpu.CompilerParams(
            dimension_semantics=("arbitrary", "arbitrary"), vmem_limit_bytes=VMEM_LIMIT),
        name="outproj_last",
    )(*args)


def _level_constants():
    T = T_SEQ
    t = jnp.arange(T)[:, None]
    s = jnp.arange(T)[None, :]
    mall = (s <= t).astype(BF16)
    x = t ^ s
    level = jnp.zeros((T, T), jnp.int32)
    for j in range(1, N_LEVELS + 1):
        level = jnp.where((x >> (j - 1)) == 1, j, level)
    level = jnp.where(s > t, -1, level)
    return mall, jnp.tile(level, (GLA_HEADS, 1))


def _prepare_weights(w_in, w_out, b_f):
    depth = w_in.shape[0]
    sizes = (FOX_W, FOX_W, FOX_W, FOX_HEADS, FOX_W, GLA_KW, GLA_KW, GLA_W, GLA_RANK, GLA_W, LRU_W, LRU_W)
    names = ("fq", "fk", "fv", "ff", "fg", "gq", "gk", "gv", "ga", "gg", "lx", "lg")
    offs, acc = {}, 0
    for nm, sz in zip(names, sizes):
        offs[nm] = acc
        acc += sz
    head_order = jnp.argsort(b_f, axis=-1)

    def head_cols(nm, width):
        return (offs[nm] + head_order[:, :, None] * width + jnp.arange(width)[None, None, :]).reshape(depth, -1)

    def plain_cols(nm, width):
        return jnp.broadcast_to(offs[nm] + jnp.arange(width)[None, :], (depth, width))

    n_pad = LANES - FOX_HEADS - GLA_RANK
    cols = jnp.concatenate([
        head_cols("fq", FOX_DH), head_cols("fk", FOX_DH), head_cols("fv", FOX_DH),
        head_cols("fg", FOX_DH), plain_cols("gq", GLA_KW), plain_cols("gk", GLA_KW), plain_cols("gv", GLA_W),
        plain_cols("gg", GLA_W), plain_cols("lx", LRU_W), plain_cols("lg", LRU_W),
        head_cols("ff", 1), plain_cols("ga", GLA_RANK), jnp.zeros((depth, n_pad), jnp.int32)], axis=1)
    scale = jnp.concatenate([
        jnp.full((FOX_W,), FOX_DH ** -0.5 * LOG2E, F32), jnp.ones((2 * FOX_W + FOX_W,), F32),
        jnp.full((GLA_KW,), GLA_DK ** -0.5, F32),
        jnp.ones((GLA_KW + 2 * GLA_W + 2 * LRU_W + FOX_HEADS + GLA_RANK,), F32), jnp.zeros((n_pad,), F32)])
    w_all = (jnp.take_along_axis(w_in, cols[:, None, :], axis=2) * scale).astype(BF16)
    rows = jnp.concatenate([
        (head_order[:, :, None] * FOX_DH + jnp.arange(FOX_DH)[None, None, :]).reshape(depth, -1),
        jnp.broadcast_to(jnp.arange(FOX_W, D_MIX)[None, :], (depth, D_MIX - FOX_W))], axis=1)
    wo_all = jnp.take_along_axis(w_out, rows[:, :, None], axis=1).astype(BF16)
    return w_all, wo_all, head_order


def kernel(x, meta, pre_g, w_in, b_f, w_a2, b_a, gla_norm_g, conv_w, conv_b, w_r, b_r, w_i, b_i, lru_lambda, w_out, post_g):
    B, S, D = x.shape
    L = S + PAD
    tm_in = 512
    tm_out = 512 if (B * L) % 512 == 0 else 256
    assert D == D_MODEL and L % T_SEQ == 0 and L % TQ == 0 and L // TK <= 2 * SUBLANES
    assert (B * L) % tm_in == 0 and S % (OUT_PARTS * PAD) == 0
    depth = w_in.shape[0]
    dt = x.dtype
    h = jnp.concatenate([jnp.zeros((B, PAD - N_META, D), dt),
                         jnp.broadcast_to(meta.astype(dt)[None], (B, N_META, D)), x], axis=1)
    h2 = h.reshape(B * L, D)

    w_all, wo_all, head_order = _prepare_weights(w_in, w_out, b_f)
    row3 = lambda a: a[:, None, :]
    wa2 = jnp.zeros((depth, LANES, GLA_KW), F32).at[:, FOX_HEADS:FOX_HEADS + GLA_RANK].set(w_a2).astype(BF16)
    bfp = jnp.zeros((depth, 1, LANES), F32).at[:, 0, 0:FOX_HEADS].set(jnp.take_along_axis(b_f, head_order, axis=1))
    wri = jnp.concatenate([w_r, w_i], axis=-1).astype(BF16)
    shared = list(_level_constants())
    stacked = [wa2, row3(b_a), bfp, row3(gla_norm_g)]
    lru_params = [conv_w, row3(conv_b), wri, row3(b_r), row3(b_i), row3(lru_lambda)]
    pre_g3, post_g3 = row3(pre_g), row3(post_g)

    out = None
    for l in range(depth):
        qkv, rest, lru = _inproj(h2, pre_g3, w_all, lru_params, l, tm_in, B, L)
        rest3 = rest.reshape(B, L, REST_W)
        ygla, ylru, ccol, crow = _seqmix(rest3, lru.reshape(B, L, LRU_OUT_W), shared, stacked, l, B, L)
        yfox = _fox(qkv.reshape(B, L, QKV_W), ccol, crow.reshape(B, FOX_HEADS // 2, 2, L), rest3, B, L)
        if l + 1 < depth:
            h2 = _outproj(yfox.reshape(B * L, FOX_W), ygla.reshape(B * L, GLA_W), ylru.reshape(B * L, LRU_W),
                          h2, wo_all, post_g3, l, tm_out)
        else:
            out = _outproj_last(yfox, ygla, ylru, h2.reshape(B, L, D), wo_all, post_g3, l, S)
    return out
```

```python
import functools

import jax
import jax.numpy as jnp
from jax import lax
from jax.experimental import pallas as pl
from jax.experimental.pallas import tpu as pltpu

F32 = jnp.float32
BF16 = jnp.bfloat16

D_MODEL = 1024
N_META = 16
PAD = 128
EPS = 1e-6
BIG = 1e30

FOX_HEADS = 8
FOX_DH = 64
FOX_W = FOX_HEADS * FOX_DH
GLA_HEADS = 4
GLA_DK = 64
GLA_DV = 128
GLA_KW = GLA_HEADS * GLA_DK
GLA_W = GLA_HEADS * GLA_DV
GLA_RANK = 16
GLA_GATE_NORM = 16.0
LRU_W = 1024
LRU_BLOCKS = 8
LRU_BS = LRU_W // LRU_BLOCKS
CONV_W = 4
LRU_C = 8.0
D_MIX = FOX_W + GLA_W + LRU_W

LANES = 128
SUBLANES = 8
VMEM_LIMIT = 56 * 1024 * 1024

QKV_W = 3 * FOX_W
R_FG = 0
R_GQ = R_FG + FOX_W
R_GK = R_GQ + GLA_KW
R_GV = R_GK + GLA_KW
R_GG = R_GV + GLA_W
R_SM = R_GG + GLA_W
REST_W = R_SM + LANES
W_G = QKV_W
W_LX = W_G + R_SM
W_LG = W_LX + LRU_W
W_SM = W_LG + LRU_W
W_ALL = W_SM + LANES
LRU_OUT_W = 3 * LRU_W

T_SEQ = 128
N_LEVELS = 7
TQ = 640
TK = 640
LOG2E = 1.4426950408889634
OUT_PARTS = 8
SKIP_LOG2 = 160.0


def _dot(a, b):
    return jnp.dot(a, b, preferred_element_type=F32)


def _dot_nt(a, b):
    return lax.dot_general(a, b, (((1,), (1,)), ((), ())), preferred_element_type=F32)


def _dot_tn(a, b):
    return lax.dot_general(a, b, (((0,), (0,)), ((), ())), preferred_element_type=F32)


def _split3(x):
    hi = x.astype(BF16)
    r = x - hi.astype(F32)
    mid = r.astype(BF16)
    lo = (r - mid.astype(F32)).astype(BF16)
    return hi, mid, lo


def _dot3(m, x):
    hi, mid, lo = _split3(x)
    return _dot(m, hi) + _dot(m, mid) + _dot(m, lo)


def _log_sigmoid(z):
    return jnp.minimum(z, 0.0) - jnp.log(1.0 + jnp.exp2(jnp.abs(z) * -LOG2E))


def _sigmoid(z):
    return 1.0 / (1.0 + jnp.exp2(z * -LOG2E))


def _silu(z):
    return z * _sigmoid(z)


def _conv_rows(x, prev, w_ref, col0):
    width = x.shape[1]
    sub = lax.broadcasted_iota(jnp.int32, (SUBLANES, 1), 0)
    groups = [prev] + [x[g * SUBLANES:(g + 1) * SUBLANES, :] for g in range(x.shape[0] // SUBLANES)]
    taps = [w_ref[0, kk:kk + 1, col0:col0 + width] for kk in range(CONV_W)]
    rolled = {}
    out = []
    for g in range(1, len(groups)):
        acc = groups[g] * taps[CONV_W - 1]
        for kk in range(1, CONV_W):
            for gi in (g - 1, g):
                if (gi, kk) not in rolled:
                    rolled[(gi, kk)] = pltpu.roll(groups[gi], kk, 0)
            acc = acc + jnp.where(sub >= kk, rolled[(g, kk)], rolled[(g - 1, kk)]) * taps[CONV_W - 1 - kk]
        out.append(acc)
    return jnp.concatenate(out, axis=0)


def _inproj_kernel(h_ref, g_ref, w_ref, cw_ref, cb_ref, wri_ref, br_ref, bi_ref, lam_ref,
                   qkv_ref, rest_ref, lru_ref, px_ref, *, n_batches, rows_per_batch):
    i = pl.program_id(0)
    tm = h_ref.shape[0]

    @pl.when(i == 0)
    def _():
        px_ref[...] = jnp.zeros_like(px_ref)

    x = h_ref[...]
    ms = jnp.mean(x * x, axis=-1, keepdims=True)
    hn = (x * lax.rsqrt(ms + EPS) * g_ref[0]).astype(BF16)
    step = 512

    def projection_stages():
        for c in range(0, QKV_W, step):
            qkv_ref[:, c:c + step] = _dot(hn, w_ref[0, :, c:c + step]).astype(BF16)
            yield
        for c in range(0, R_SM, step):
            rest_ref[:, c:c + step] = _dot(hn, w_ref[0, :, W_G + c:W_G + c + step])
            yield
        rest_ref[:, R_SM:REST_W] = _dot(hn, w_ref[0, :, W_SM:W_ALL])

    def lru_stages():
        r = i * tm + lax.broadcasted_iota(jnp.int32, (tm, 1), 0)
        invalid = (r < 0)
        for b in range(n_batches):
            invalid = invalid | ((r >= b * rows_per_batch) & (r < b * rows_per_batch + PAD - N_META))
        valid = jnp.where(invalid, 0.0, 1.0)
        soft = jnp.maximum(-lam_ref[0], 0.0) + jnp.log(1.0 + jnp.exp(-jnp.abs(lam_ref[0])))
        for c in range(0, LRU_W, step):
            lx = _dot(hn, w_ref[0, :, W_LX + c:W_LX + c + step])
            yield
            xc = _conv_rows(lx, px_ref[:, c:c + step], cw_ref, c)
            px_ref[:, c:c + step] = lx[tm - SUBLANES:tm, :]
            xc = (xc + cb_ref[0, :, c:c + step]) * valid
            xcb = xc.astype(BF16)
            yield
            for blk in range(step // LRU_BS):
                sl = slice(blk * LRU_BS, (blk + 1) * LRU_BS)
                ch = slice(c + blk * LRU_BS, c + (blk + 1) * LRU_BS)
                ri = _dot(xcb[:, sl], wri_ref[0, (c // LRU_BS) + blk])
                rg = _sigmoid(ri[:, 0:LRU_BS] + br_ref[0, :, ch])
                ig = _sigmoid(ri[:, LRU_BS:2 * LRU_BS] + bi_ref[0, :, ch])
                la = -LRU_C * rg * soft[:, ch]
                th = jnp.tanh(la)
                w = -2.0 * th / (1.0 - th)
                root = jnp.where(w > 0.0, w * lax.rsqrt(w), 0.0)
                lru_ref[:, c + blk * LRU_BS:c + (blk + 1) * LRU_BS] = jnp.exp(la)
                lru_ref[:, LRU_W + c + blk * LRU_BS:LRU_W + c + (blk + 1) * LRU_BS] = root * (ig * xc[:, sl])
                if blk % 2 == 1:
                    yield
            lg = _dot(hn, w_ref[0, :, W_LG + c:W_LG + c + step])
            lru_ref[:, 2 * LRU_W + c:2 * LRU_W + c + step] = _silu(lg)
            yield

    streams = [lru_stages(), projection_stages()]
    while streams:
        for gen in list(streams):
            if next(gen, StopIteration) is StopIteration:
                streams.remove(gen)


def _inproj(h2, g_all, w_all, lru_params, layer, tm, n_batches, rows_per_batch):
    n = h2.shape[0]
    stacked_specs = [pl.BlockSpec((1,) + c.shape[1:], functools.partial(lambda nd, i: (layer,) + (0,) * nd, c.ndim - 1))
                     for c in lru_params]
    return pl.pallas_call(
        functools.partial(_inproj_kernel, n_batches=n_batches, rows_per_batch=rows_per_batch),
        grid=(n // tm,),
        in_specs=[
            pl.BlockSpec((tm, D_MODEL), lambda i: (i, 0)),
            pl.BlockSpec((1, 1, D_MODEL), lambda i: (layer, 0, 0)),
            pl.BlockSpec((1, D_MODEL, W_ALL), lambda i: (layer, 0, 0)),
        ] + stacked_specs,
        out_specs=[
            pl.BlockSpec((tm, QKV_W), lambda i: (i, 0)),
            pl.BlockSpec((tm, REST_W), lambda i: (i, 0)),
            pl.BlockSpec((tm, LRU_OUT_W), lambda i: (i, 0)),
        ],
        out_shape=[
            jax.ShapeDtypeStruct((n, QKV_W), BF16),
            jax.ShapeDtypeStruct((n, REST_W), F32),
            jax.ShapeDtypeStruct((n, LRU_OUT_W), F32),
        ],
        scratch_shapes=[pltpu.VMEM((SUBLANES, LRU_W), F32)],
        compiler_params=pltpu.CompilerParams(
            dimension_semantics=("arbitrary",), vmem_limit_bytes=VMEM_LIMIT),
        name="inproj",
    )(h2, g_all, w_all, *lru_params)


def _level_reference(b, j, sub):
    T = b.shape[0]
    half = 1 << (j - 1)
    parts = []
    if j > 3:
        for blk in range(T >> j):
            r = (blk << j) + half - 1
            parts.append(jnp.broadcast_to(b[r:r + 1, :], (1 << j, b.shape[1])))
        return jnp.concatenate(parts, axis=0)
    for g in range(T // SUBLANES):
        x = b[g * SUBLANES:(g + 1) * SUBLANES, :]
        if j == 3:
            parts.append(jnp.broadcast_to(x[half - 1:half, :], x.shape))
            continue
        ref = x
        for off in range(-(half - 1), half + 1):
            if off != 0:
                pick = (sub & ((1 << j) - 1)) == (half - 1 + off)
                ref = jnp.where(pick, pltpu.roll(x, off % SUBLANES, 0), ref)
        parts.append(ref)
    return jnp.concatenate(parts, axis=0)


def _seqmix_kernel(gqk_ref, gv_ref, gg_ref, sm_ref, a_ref, u_ref, sl_ref,
                   mall_ref, lv_ref, wa2_ref, ba_ref, bf_ref, gng_ref,
                   ygla_ref, ylru_ref, ccol_ref, crow_ref,
                   st_ref, cc_ref, hc_ref):
    t = pl.program_id(1)
    T = T_SEQ

    @pl.when(t == 0)
    def _():
        st_ref[...] = jnp.zeros_like(st_ref)
        cc_ref[...] = jnp.zeros_like(cc_ref)
        hc_ref[...] = jnp.zeros_like(hc_ref)

    row = lax.broadcasted_iota(jnp.int32, (T, 1), 0)
    valid = ((t * T + row) >= (PAD - N_META)).astype(F32)
    tri = mall_ref[...]

    groups = T // SUBLANES
    sub = lax.broadcasted_iota(jnp.int32, (SUBLANES, 1), 0)

    def lru_stages():
        h_prev = hc_ref[...]
        for g in range(groups):
            rows = slice(g * SUBLANES, (g + 1) * SUBLANES)
            ag, ug = a_ref[0, rows, :], u_ref[0, rows, :]
            d = 1
            while d < SUBLANES:
                keep = sub >= d
                a_sh = jnp.where(keep, pltpu.roll(ag, d, 0), 1.0)
                u_sh = jnp.where(keep, pltpu.roll(ug, d, 0), 0.0)
                ug = ag * u_sh + ug
                ag = ag * a_sh
                d *= 2
            hg = ag * h_prev + ug
            h_prev = hg[SUBLANES - 1:SUBLANES, :]
            ylru_ref[0, rows, :] = (hg * sl_ref[0, rows, :] * valid[rows, :]).astype(BF16)
            if g % 2 == 1:
                yield
        hc_ref[...] = h_prev

    def gla_stages():
        sm = sm_ref[0]
        log_f = _log_sigmoid(sm + bf_ref[0]) * valid
        z = _dot(sm.astype(BF16), wa2_ref[0]) + ba_ref[0]
        log_a = _log_sigmoid(z) / GLA_GATE_NORM
        yield
        c = _dot3(tri, log_f) + cc_ref[...]
        cc_ref[...] = c[T - 1:T, :]
        c2 = c * LOG2E
        ccol_ref[0] = c2
        c_keys = jnp.where(valid > 0.0, c2, BIG)
        crow_ref[0] = c_keys.T[0:FOX_HEADS, :]
        b = _dot3(tri, log_a)
        yield
        gqk = gqk_ref[0]
        q = gqk[:, 0:GLA_KW]
        k = gqk[:, GLA_KW:2 * GLA_KW] * valid
        v = gv_ref[0].astype(BF16)
        lane_k = lax.broadcasted_iota(jnp.int32, (1, GLA_KW), 1)
        head_masks = [(lane_k >= hh * GLA_DK) & (lane_k < (hh + 1) * GLA_DK) for hh in range(GLA_HEADS)]

        def stack_heads(x):
            return jnp.concatenate([jnp.where(mk, x, 0.0).astype(BF16) for mk in head_masks], axis=0)

        lv = lv_ref[...]
        a_st = jnp.where(lv == 0, _dot_nt(stack_heads(q), k.astype(BF16)), 0.0)
        yield
        for j in range(1, N_LEVELS + 1):
            bref = _level_reference(b, j, sub)
            e = jnp.exp2(jnp.abs(b - bref) * -LOG2E)
            a_st = a_st + jnp.where(lv == j, _dot_nt(stack_heads(q * e), (k * e).astype(BF16)), 0.0)
            yield
        a_st = a_st.astype(BF16)
        b_last = b[T - 1:T, :]
        qd_st = stack_heads(q * jnp.exp(b))
        kd = (k * jnp.exp(b_last - b)).astype(BF16)
        st = st_ref[...]
        st_bf = st.astype(BF16)
        st_new = st * jnp.exp(b_last)
        gg = gg_ref[0]
        for hh in range(GLA_HEADS):
            vh = v[:, hh * GLA_DV:(hh + 1) * GLA_DV]
            o = _dot(a_st[hh * T:(hh + 1) * T], vh) + _dot_nt(qd_st[hh * T:(hh + 1) * T], st_bf)
            o = o * lax.rsqrt(jnp.mean(o * o, axis=-1, keepdims=True) + EPS)
            o = o * gng_ref[0, :, hh * GLA_DV:(hh + 1) * GLA_DV]
            y = o * _silu(gg[:, hh * GLA_DV:(hh + 1) * GLA_DV]) * valid
            ygla_ref[0, :, hh * GLA_DV:(hh + 1) * GLA_DV] = y.astype(BF16)
            st_new = st_new + jnp.where(head_masks[hh], _dot_tn(vh, kd), 0.0)
            yield
        st_ref[...] = st_new

    streams = [gla_stages(), lru_stages()]
    while streams:
        for gen in list(streams):
            if next(gen, StopIteration) is StopIteration:
                streams.remove(gen)


def _seqmix(rest3, lru3, shared, stacked, layer, B, L):
    T = T_SEQ
    nt = L // T
    consts = list(shared) + list(stacked)
    const_specs = [pl.BlockSpec(c.shape, functools.partial(lambda nd, b, t: (0,) * nd, c.ndim)) for c in shared]
    const_specs += [pl.BlockSpec((1,) + c.shape[1:], functools.partial(lambda nd, b, t: (layer,) + (0,) * nd, c.ndim - 1))
                    for c in stacked]
    return pl.pallas_call(
        _seqmix_kernel,
        grid=(B, nt),
        in_specs=[
            pl.BlockSpec((1, T, 2 * GLA_KW), lambda b, t: (b, t, R_GQ // (2 * GLA_KW))),
            pl.BlockSpec((1, T, GLA_W), lambda b, t: (b, t, R_GV // GLA_W)),
            pl.BlockSpec((1, T, GLA_W), lambda b, t: (b, t, R_GG // GLA_W)),
            pl.BlockSpec((1, T, LANES), lambda b, t: (b, t, R_SM // LANES)),
            pl.BlockSpec((1, T, LRU_W), lambda b, t: (b, t, 0)),
            pl.BlockSpec((1, T, LRU_W), lambda b, t: (b, t, 1)),
            pl.BlockSpec((1, T, LRU_W), lambda b, t: (b, t, 2)),
        ] + const_specs,
        out_specs=[
            pl.BlockSpec((1, T, GLA_W), lambda b, t: (b, t, 0)),
            pl.BlockSpec((1, T, LRU_W), lambda b, t: (b, t, 0)),
            pl.BlockSpec((1, T, LANES), lambda b, t: (b, t, 0)),
            pl.BlockSpec((1, FOX_HEADS, T), lambda b, t: (b, 0, t)),
        ],
        out_shape=[
            jax.ShapeDtypeStruct((B, L, GLA_W), BF16),
            jax.ShapeDtypeStruct((B, L, LRU_W), BF16),
            jax.ShapeDtypeStruct((B, L, LANES), F32),
            jax.ShapeDtypeStruct((B, FOX_HEADS, L), F32),
        ],
        scratch_shapes=[
            pltpu.VMEM((GLA_DV, GLA_KW), F32),
            pltpu.VMEM((1, LANES), F32),
            pltpu.VMEM((1, LRU_W), F32),
        ],
        compiler_params=pltpu.CompilerParams(
            dimension_semantics=("arbitrary", "arbitrary"), vmem_limit_bytes=VMEM_LIMIT),
        name="seqmix",
    )(rest3, rest3, rest3, rest3, lru3, lru3, lru3, *consts)


def _head_sqnorm_max(x, first):
    x2 = x.astype(F32)
    x2 = x2 * x2
    n0 = jnp.sum(jnp.where(first, x2, 0.0), axis=-1, keepdims=True)
    n1 = jnp.sum(jnp.where(first, 0.0, x2), axis=-1, keepdims=True)
    return jnp.max(n0, axis=0, keepdims=True), jnp.max(n1, axis=0, keepdims=True)


def _fox_kernel(q_ref, k_ref, v_ref, ccol_ref, crow_ref, fg_ref, bias_ref, o_ref, kst_ref, sa_ref, sb_ref, m_ref, acc_ref):
    hp = pl.program_id(1)
    i = pl.program_id(2)
    nkb = k_ref.shape[1] // TK
    lane = lax.broadcasted_iota(jnp.int32, (1, LANES), 1)
    first = lane < FOX_DH

    @pl.when(i == 0)
    def _():
        kst_ref[...] = jnp.zeros_like(kst_ref)
        for j in range(nkb):
            n0, n1 = _head_sqnorm_max(k_ref[0, j * TK:(j + 1) * TK, :], first)
            cmin = jnp.min(crow_ref[0, 0, :, j * TK:(j + 1) * TK], axis=-1, keepdims=True)
            kst_ref[j:j + 1, :] = jnp.where(lane == 0, n0, jnp.where(lane == 1, n1, jnp.where(
                lane == 2, cmin[0:1], jnp.where(lane == 3, cmin[1:2], 0.0))))

    q = q_ref[0]
    zero = jnp.zeros_like(q)
    qh = (jnp.where(first, q, zero), jnp.where(first, zero, q))
    ccol = ccol_ref[0]
    cq = tuple(jnp.sum(jnp.where(lane == 2 * hp + hh, ccol, 0.0), axis=-1, keepdims=True) for hh in range(2))
    lane2 = lax.broadcasted_iota(jnp.int32, (1, 2 * LANES), 1)
    sel0 = (lane2 < FOX_DH) | (lane2 == LANES)
    ones_blk = jnp.broadcast_to((lane < 2).astype(BF16), (TK, LANES))

    def scores(j, dst):
        off = pl.multiple_of(j * TK, LANES)
        k = k_ref[0, pl.ds(off, TK), :]
        ck = crow_ref[0, 0, :, pl.ds(off, TK)]
        for hh in range(2):
            dst[hh] = _dot_nt(qh[hh], k) + cq[hh] - ck[hh:hh + 1, :]

    def consume(j, src, diag):
        off = pl.multiple_of(j * TK, LANES)
        vext = jnp.concatenate([v_ref[0, pl.ds(off, TK), :], ones_blk], axis=1)
        alphas, pvs = [], []
        for hh in range(2):
            s = src[hh]
            if diag:
                s = s + bias_ref[...]
            m_old = m_ref[hh]
            m_new = jnp.maximum(m_old, jnp.max(s, axis=-1, keepdims=True))
            m_ref[hh] = m_new
            alphas.append(jnp.exp2(m_old - m_new))
            pvs.append(_dot(jnp.exp2(s - m_new).astype(BF16), vext))
        acc_ref[...] = acc_ref[...] * jnp.where(sel0, alphas[0], alphas[1]) + jnp.where(sel0, pvs[0], pvs[1])

    def scores_both(j, dst):
        off = pl.multiple_of(j * TK, LANES)
        k = k_ref[0, pl.ds(off, TK), :]
        kz = jnp.zeros_like(k)
        k2 = jnp.concatenate([jnp.where(first, k, kz), jnp.where(first, kz, k)], axis=0)
        ck = crow_ref[0, 0, :, pl.ds(off, TK)]
        s2 = _dot_nt(q, k2)
        for hh in range(2):
            dst[hh] = s2[:, hh * TK:(hh + 1) * TK] + cq[hh] - ck[hh:hh + 1, :]

    def consume_both(j, src):
        off = pl.multiple_of(j * TK, LANES)
        v = v_ref[0, pl.ds(off, TK), :]
        vz = jnp.zeros_like(v)
        v2 = jnp.concatenate([
            jnp.concatenate([jnp.where(first, v, vz), jnp.broadcast_to((lane == 0).astype(BF16), (TK, LANES))], axis=1),
            jnp.concatenate([jnp.where(first, vz, v), jnp.broadcast_to((lane == 1).astype(BF16), (TK, LANES))], axis=1),
        ], axis=0)
        alphas, ps = [], []
        for hh in range(2):
            s = src[hh]
            m_old = m_ref[hh]
            m_new = jnp.maximum(m_old, jnp.max(s, axis=-1, keepdims=True))
            m_ref[hh] = m_new
            alphas.append(jnp.exp2(m_old - m_new))
            ps.append(jnp.exp2(s - m_new).astype(BF16))
        pv = _dot(jnp.concatenate(ps, axis=1), v2)
        acc_ref[...] = acc_ref[...] * jnp.where(sel0, alphas[0], alphas[1]) + pv

    m_ref[...] = jnp.full(m_ref.shape, -BIG, F32)
    acc_ref[...] = jnp.zeros(acc_ref.shape, F32)
    scores(i, sa_ref)
    scores(jnp.maximum(i - 1, 0), sb_ref)
    consume(i, sa_ref, True)

    qn = _head_sqnorm_max(q, first)
    kst = kst_ref[...]
    rowj = lax.broadcasted_iota(jnp.int32, (kst.shape[0], 1), 0)
    starts = []
    for hh in range(2):
        kn = kst[:, hh:hh + 1] * 1.01
        cmin = kst[:, 2 + hh:3 + hh]
        m_low = jnp.min(m_ref[hh], axis=0, keepdims=True)
        ub = jnp.max(cq[hh], axis=0, keepdims=True) - cmin + jnp.sqrt(qn[hh] * kn) - m_low
        skippable = (ub <= -SKIP_LOG2) & (rowj < i)
        starts.append(jnp.min(jnp.where(skippable, kst.shape[0], rowj)))
    j0 = jnp.minimum(starts[0], starts[1])
    nb = i - j0

    def blk(kk):
        return jnp.where(kk == 0, i - 1, j0 + kk - 1)

    def pair(n, carry):
        scores_both(blk(2 * n + 1), sa_ref)
        consume_both(blk(2 * n), sb_ref)
        scores_both(blk(2 * n + 2), sb_ref)
        consume_both(blk(2 * n + 1), sa_ref)
        return carry

    npairs = jnp.maximum(nb - 1, 0) // 2
    lax.fori_loop(0, npairs, pair, 0)

    @pl.when((nb > 0) & (nb % 2 == 0))
    def _():
        scores_both(blk(nb - 1), sa_ref)
        consume_both(blk(nb - 2), sb_ref)
        consume_both(blk(nb - 1), sa_ref)

    @pl.when(nb % 2 == 1)
    def _():
        consume_both(blk(nb - 1), sb_ref)

    acc = acc_ref[...]
    o = acc[:, 0:LANES] / jnp.where(first, acc[:, LANES:LANES + 1], acc[:, LANES + 1:LANES + 2])
    row = lax.broadcasted_iota(jnp.int32, (TQ, 1), 0)
    valid = ((i * TQ + row) >= (PAD - N_META)).astype(F32)
    o_ref[0] = (o * _silu(fg_ref[0]) * valid).astype(BF16)


def _fox(qkv3, ccol, crow4, rest3, B, L):
    nq = L // TQ
    causal_bias = jnp.where(jnp.arange(TK)[None, :] <= jnp.arange(TQ)[:, None], 0.0, -BIG).astype(F32)
    hpairs = FOX_HEADS // 2
    return pl.pallas_call(
        _fox_kernel,
        grid=(B, hpairs, nq),
        in_specs=[
            pl.BlockSpec((1, TQ, LANES), lambda b, h, i: (b, i, h)),
            pl.BlockSpec((1, L, LANES), lambda b, h, i: (b, 0, FOX_W // LANES + h)),
            pl.BlockSpec((1, L, LANES), lambda b, h, i: (b, 0, 2 * FOX_W // LANES + h)),
            pl.BlockSpec((1, TQ, LANES), lambda b, h, i: (b, i, 0)),
            pl.BlockSpec((1, 1, 2, L), lambda b, h, i: (b, h, 0, 0)),
            pl.BlockSpec((1, TQ, LANES), lambda b, h, i: (b, i, R_FG // LANES + h)),
            pl.BlockSpec((TQ, TK), lambda b, h, i: (0, 0)),
        ],
        out_specs=pl.BlockSpec((1, TQ, LANES), lambda b, h, i: (b, i, h)),
        out_shape=jax.ShapeDtypeStruct((B, L, FOX_W), BF16),
        scratch_shapes=[
            pltpu.VMEM((2 * SUBLANES, LANES), F32),
            pltpu.VMEM((2, TQ, TK), F32),
            pltpu.VMEM((2, TQ, TK), F32),
            pltpu.VMEM((2, TQ, 1), F32),
            pltpu.VMEM((TQ, 2 * LANES), F32),
        ],
        compiler_params=pltpu.CompilerParams(
            dimension_semantics=("arbitrary", "arbitrary", "arbitrary"), vmem_limit_bytes=VMEM_LIMIT),
        name="fox",
    )(qkv3, qkv3, qkv3, ccol, crow4, rest3, causal_bias)


def _outproj_math(yf, yg, yl, h, w_ref, g_ref):
    acc = _dot(yf, w_ref[0, 0:FOX_W, :])
    acc = acc + _dot(yg, w_ref[0, FOX_W:FOX_W + GLA_W, :])
    acc = acc + _dot(yl, w_ref[0, FOX_W + GLA_W:D_MIX, :])
    ms = jnp.mean(acc * acc, axis=-1, keepdims=True)
    return h + acc * lax.rsqrt(ms + EPS) * g_ref[0]


def _outproj_kernel(yf_ref, yg_ref, yl_ref, h_ref, w_ref, g_ref, o_ref):
    o_ref[...] = _outproj_math(yf_ref[...], yg_ref[...], yl_ref[...], h_ref[...], w_ref, g_ref)


def _outproj(yf, yg, yl, h2, w_all, g_all, layer, tm):
    n = h2.shape[0]
    return pl.pallas_call(
        _outproj_kernel,
        grid=(n // tm,),
        in_specs=[
            pl.BlockSpec((tm, FOX_W), lambda i: (i, 0)),
            pl.BlockSpec((tm, GLA_W), lambda i: (i, 0)),
            pl.BlockSpec((tm, LRU_W), lambda i: (i, 0)),
            pl.BlockSpec((tm, D_MODEL), lambda i: (i, 0)),
            pl.BlockSpec((1, D_MIX, D_MODEL), lambda i: (layer, 0, 0)),
            pl.BlockSpec((1, 1, D_MODEL), lambda i: (layer, 0, 0)),
        ],
        out_specs=pl.BlockSpec((tm, D_MODEL), lambda i: (i, 0)),
        out_shape=jax.ShapeDtypeStruct((n, D_MODEL), F32),
        compiler_params=pltpu.CompilerParams(
            dimension_semantics=("arbitrary",), vmem_limit_bytes=VMEM_LIMIT),
        name="outproj",
    )(yf, yg, yl, h2, w_all, g_all)


def _outproj_last_kernel(*refs):
    n = OUT_PARTS
    yf, yg, yl, h = [jnp.concatenate([r[0] for r in refs[a * n:(a + 1) * n]], axis=0) for a in range(4)]
    w_ref, g_ref, o_ref = refs[4 * n:]
    o_ref[0] = _outproj_math(yf, yg, yl, h, w_ref, g_ref)


def _outproj_last(yf3, yg3, yl3, h3, w_all, g_all, layer, S):
    B = h3.shape[0]
    tm = OUT_PARTS * PAD

    def part_specs(width):
        return [pl.BlockSpec((1, PAD, width), functools.partial(lambda r, b, j: (b, 1 + OUT_PARTS * j + r, 0), r))
                for r in range(OUT_PARTS)]

    in_specs = part_specs(FOX_W) + part_specs(GLA_W) + part_specs(LRU_W) + part_specs(D_MODEL) + [
        pl.BlockSpec((1, D_MIX, D_MODEL), lambda b, j: (layer, 0, 0)),
        pl.BlockSpec((1, 1, D_MODEL), lambda b, j: (layer, 0, 0)),
    ]
    args = [yf3] * OUT_PARTS + [yg3] * OUT_PARTS + [yl3] * OUT_PARTS + [h3] * OUT_PARTS + [w_all, g_all]
    return pl.pallas_call(
        _outproj_last_kernel,
        grid=(B, S // tm),
        in_specs=in_specs,
        out_specs=pl.BlockSpec((1, tm, D_MODEL), lambda b, j: (b, j, 0)),
        out_shape=jax.ShapeDtypeStruct((B, S, D_MODEL), F32),
        compiler_params=pltpu.CompilerParams(
            dimension_semantics=("arbitrary", "arbitrary"), vmem_limit_bytes=VMEM_LIMIT),
        name="outproj_last",
    )(*args)


def _level_constants():
    T = T_SEQ
    t = jnp.arange(T)[:, None]
    s = jnp.arange(T)[None, :]
    mall = (s <= t).astype(BF16)
    x = t ^ s
    level = jnp.zeros((T, T), jnp.int32)
    for j in range(1, N_LEVELS + 1):
        level = jnp.where((x >> (j - 1)) == 1, j, level)
    level = jnp.where(s > t, -1, level)
    return mall, jnp.tile(level, (GLA_HEADS, 1))


def _prepare_weights(w_in, w_out, b_f):
    depth = w_in.shape[0]
    sizes = (FOX_W, FOX_W, FOX_W, FOX_HEADS, FOX_W, GLA_KW, GLA_KW, GLA_W, GLA_RANK, GLA_W, LRU_W, LRU_W)
    names = ("fq", "fk", "fv", "ff", "fg", "gq", "gk", "gv", "ga", "gg", "lx", "lg")
    offs, acc = {}, 0
    for nm, sz in zip(names, sizes):
        offs[nm] = acc
        acc += sz
    head_order = jnp.argsort(b_f, axis=-1)

    def head_cols(nm, width):
        return (offs[nm] + head_order[:, :, None] * width + jnp.arange(width)[None, None, :]).reshape(depth, -1)

    def plain_cols(nm, width):
        return jnp.broadcast_to(offs[nm] + jnp.arange(width)[None, :], (depth, width))

    n_pad = LANES - FOX_HEADS - GLA_RANK
    cols = jnp.concatenate([
        head_cols("fq", FOX_DH), head_cols("fk", FOX_DH), head_cols("fv", FOX_DH),
        head_cols("fg", FOX_DH), plain_cols("gq", GLA_KW), plain_cols("gk", GLA_KW), plain_cols("gv", GLA_W),
        plain_cols("gg", GLA_W), plain_cols("lx", LRU_W), plain_cols("lg", LRU_W),
        head_cols("ff", 1), plain_cols("ga", GLA_RANK), jnp.zeros((depth, n_pad), jnp.int32)], axis=1)
    scale = jnp.concatenate([
        jnp.full((FOX_W,), FOX_DH ** -0.5 * LOG2E, F32), jnp.ones((2 * FOX_W + FOX_W,), F32),
        jnp.full((GLA_KW,), GLA_DK ** -0.5, F32),
        jnp.ones((GLA_KW + 2 * GLA_W + 2 * LRU_W + FOX_HEADS + GLA_RANK,), F32), jnp.zeros((n_pad,), F32)])
    w_all = (jnp.take_along_axis(w_in, cols[:, None, :], axis=2) * scale).astype(BF16)
    rows = jnp.concatenate([
        (head_order[:, :, None] * FOX_DH + jnp.arange(FOX_DH)[None, None, :]).reshape(depth, -1),
        jnp.broadcast_to(jnp.arange(FOX_W, D_MIX)[None, :], (depth, D_MIX - FOX_W))], axis=1)
    wo_all = jnp.take_along_axis(w_out, rows[:, :, None], axis=1).astype(BF16)
    return w_all, wo_all, head_order


def kernel(x, meta, pre_g, w_in, b_f, w_a2, b_a, gla_norm_g, conv_w, conv_b, w_r, b_r, w_i, b_i, lru_lambda, w_out, post_g):
    B, S, D = x.shape
    L = S + PAD
    tm_in = 512
    tm_out = 1664 if (B * L) % 1664 == 0 else 512
    assert D == D_MODEL and L % T_SEQ == 0 and L % TQ == 0 and L // TK <= 2 * SUBLANES
    assert (B * L) % tm_in == 0 and S % (OUT_PARTS * PAD) == 0
    depth = w_in.shape[0]
    dt = x.dtype
    h = jnp.concatenate([jnp.zeros((B, PAD - N_META, D), dt),
                         jnp.broadcast_to(meta.astype(dt)[None], (B, N_META, D)), x], axis=1)
    h2 = h.reshape(B * L, D)

    w_all, wo_all, head_order = _prepare_weights(w_in, w_out, b_f)
    row3 = lambda a: a[:, None, :]
    wa2 = jnp.zeros((depth, LANES, GLA_KW), F32).at[:, FOX_HEADS:FOX_HEADS + GLA_RANK].set(w_a2).astype(BF16)
    bfp = jnp.zeros((depth, 1, LANES), F32).at[:, 0, 0:FOX_HEADS].set(jnp.take_along_axis(b_f, head_order, axis=1))
    wri = jnp.concatenate([w_r, w_i], axis=-1).astype(BF16)
    shared = list(_level_constants())
    stacked = [wa2, row3(b_a), bfp, row3(gla_norm_g)]
    lru_params = [conv_w, row3(conv_b), wri, row3(b_r), row3(b_i), row3(lru_lambda)]
    pre_g3, post_g3 = row3(pre_g), row3(post_g)

    out = None
    for l in range(depth):
        qkv, rest, lru = _inproj(h2, pre_g3, w_all, lru_params, l, tm_in, B, L)
        rest3 = rest.reshape(B, L, REST_W)
        ygla, ylru, ccol, crow = _seqmix(rest3, lru.reshape(B, L, LRU_OUT_W), shared, stacked, l, B, L)
        yfox = _fox(qkv.reshape(B, L, QKV_W), ccol, crow.reshape(B, FOX_HEADS // 2, 2, L), rest3, B, L)
        if l + 1 < depth:
            h2 = _outproj(yfox.reshape(B * L, FOX_W), ygla.reshape(B * L, GLA_W), ylru.reshape(B * L, LRU_W),
                          h2, wo_all, post_g3, l, tm_out)
        else:
            out = _outproj_last(yfox, ygla, ylru, h2.reshape(B, L, D), wo_all, post_g3, l, S)
    return out
```

```python
import functools

import jax
import jax.numpy as jnp
from jax import lax
from jax.experimental import pallas as pl
from jax.experimental.pallas import tpu as pltpu

F32 = jnp.float32
BF16 = jnp.bfloat16

D_MODEL = 1024
N_META = 16
PAD = 128
EPS = 1e-6
BIG = 1e30

FOX_HEADS = 8
FOX_DH = 64
FOX_W = FOX_HEADS * FOX_DH
GLA_HEADS = 4
GLA_DK = 64
GLA_DV = 128
GLA_KW = GLA_HEADS * GLA_DK
GLA_W = GLA_HEADS * GLA_DV
GLA_RANK = 16
GLA_GATE_NORM = 16.0
LRU_W = 1024
LRU_BLOCKS = 8
LRU_BS = LRU_W // LRU_BLOCKS
CONV_W = 4
LRU_C = 8.0
D_MIX = FOX_W + GLA_W + LRU_W

LANES = 128
SUBLANES = 8
VMEM_LIMIT = 56 * 1024 * 1024

QKV_W = 3 * FOX_W
R_FG = 0
R_GQ = R_FG + FOX_W
R_GK = R_GQ + GLA_KW
R_GV = R_GK + GLA_KW
R_GG = R_GV + GLA_W
R_SM = R_GG + GLA_W
REST_W = R_SM + LANES
W_G = QKV_W
W_LX = W_G + R_SM
W_LG = W_LX + LRU_W
W_SM = W_LG + LRU_W
W_ALL = W_SM + LANES
LRU_OUT_W = 3 * LRU_W

T_SEQ = 128
N_LEVELS = 7
TQ = 640
TK = 640
LOG2E = 1.4426950408889634
OUT_PARTS = 4
SKIP_LOG2 = 160.0


def _dot(a, b):
    return jnp.dot(a, b, preferred_element_type=F32)


def _dot_nt(a, b):
    return lax.dot_general(a, b, (((1,), (1,)), ((), ())), preferred_element_type=F32)


def _dot_tn(a, b):
    return lax.dot_general(a, b, (((0,), (0,)), ((), ())), preferred_element_type=F32)


def _split3(x):
    hi = x.astype(BF16)
    r = x - hi.astype(F32)
    mid = r.astype(BF16)
    lo = (r - mid.astype(F32)).astype(BF16)
    return hi, mid, lo


def _dot3(m, x):
    hi, mid, lo = _split3(x)
    return _dot(m, hi) + _dot(m, mid) + _dot(m, lo)


def _log_sigmoid(z):
    return jnp.minimum(z, 0.0) - jnp.log(1.0 + jnp.exp2(jnp.abs(z) * -LOG2E))


def _sigmoid(z):
    return 1.0 / (1.0 + jnp.exp2(z * -LOG2E))


def _silu(z):
    return z * _sigmoid(z)


def _conv_rows(x, prev, w_ref, col0):
    width = x.shape[1]
    sub = lax.broadcasted_iota(jnp.int32, (SUBLANES, 1), 0)
    groups = [prev] + [x[g * SUBLANES:(g + 1) * SUBLANES, :] for g in range(x.shape[0] // SUBLANES)]
    taps = [w_ref[0, kk:kk + 1, col0:col0 + width] for kk in range(CONV_W)]
    rolled = {}
    out = []
    for g in range(1, len(groups)):
        acc = groups[g] * taps[CONV_W - 1]
        for kk in range(1, CONV_W):
            for gi in (g - 1, g):
                if (gi, kk) not in rolled:
                    rolled[(gi, kk)] = pltpu.roll(groups[gi], kk, 0)
            acc = acc + jnp.where(sub >= kk, rolled[(g, kk)], rolled[(g - 1, kk)]) * taps[CONV_W - 1 - kk]
        out.append(acc)
    return jnp.concatenate(out, axis=0)


def _inproj_kernel(h_ref, g_ref, w_ref, cw_ref, cb_ref, wri_ref, br_ref, bi_ref, lam_ref,
                   qkv_ref, rest_ref, lru_ref, px_ref, *, n_batches, rows_per_batch):
    i = pl.program_id(0)
    tm = h_ref.shape[0]

    @pl.when(i == 0)
    def _():
        px_ref[...] = jnp.zeros_like(px_ref)

    x = h_ref[...]
    ms = jnp.mean(x * x, axis=-1, keepdims=True)
    hn = (x * lax.rsqrt(ms + EPS) * g_ref[0]).astype(BF16)
    step = 512

    def projection_stages():
        for c in range(0, QKV_W, step):
            qkv_ref[:, c:c + step] = _dot(hn, w_ref[0, :, c:c + step]).astype(BF16)
            yield
        for c in range(0, R_SM, step):
            rest_ref[:, c:c + step] = _dot(hn, w_ref[0, :, W_G + c:W_G + c + step])
            yield
        rest_ref[:, R_SM:REST_W] = _dot(hn, w_ref[0, :, W_SM:W_ALL])

    def lru_stages():
        r = i * tm + lax.broadcasted_iota(jnp.int32, (tm, 1), 0)
        invalid = (r < 0)
        for b in range(n_batches):
            invalid = invalid | ((r >= b * rows_per_batch) & (r < b * rows_per_batch + PAD - N_META))
        valid = jnp.where(invalid, 0.0, 1.0)
        soft = jnp.maximum(-lam_ref[0], 0.0) + jnp.log(1.0 + jnp.exp(-jnp.abs(lam_ref[0])))
        for c in range(0, LRU_W, step):
            lx = _dot(hn, w_ref[0, :, W_LX + c:W_LX + c + step])
            yield
            xc = _conv_rows(lx, px_ref[:, c:c + step], cw_ref, c)
            px_ref[:, c:c + step] = lx[tm - SUBLANES:tm, :]
            xc = (xc + cb_ref[0, :, c:c + step]) * valid
            xcb = xc.astype(BF16)
            yield
            for blk in range(step // LRU_BS):
                sl = slice(blk * LRU_BS, (blk + 1) * LRU_BS)
                ch = slice(c + blk * LRU_BS, c + (blk + 1) * LRU_BS)
                ri = _dot(xcb[:, sl], wri_ref[0, (c // LRU_BS) + blk])
                rg = _sigmoid(ri[:, 0:LRU_BS] + br_ref[0, :, ch])
                ig = _sigmoid(ri[:, LRU_BS:2 * LRU_BS] + bi_ref[0, :, ch])
                la = -LRU_C * rg * soft[:, ch]
                th = jnp.tanh(la)
                w = -2.0 * th / (1.0 - th)
                root = jnp.where(w > 0.0, w * lax.rsqrt(w), 0.0)
                lru_ref[:, c + blk * LRU_BS:c + (blk + 1) * LRU_BS] = jnp.exp(la)
                lru_ref[:, LRU_W + c + blk * LRU_BS:LRU_W + c + (blk + 1) * LRU_BS] = root * (ig * xc[:, sl])
                if blk % 2 == 1:
                    yield
            lg = _dot(hn, w_ref[0, :, W_LG + c:W_LG + c + step])
            lru_ref[:, 2 * LRU_W + c:2 * LRU_W + c + step] = _silu(lg)
            yield

    streams = [lru_stages(), projection_stages()]
    while streams:
        for gen in list(streams):
            if next(gen, StopIteration) is StopIteration:
                streams.remove(gen)


def _inproj(h2, g_all, w_all, lru_params, layer, tm, n_batches, rows_per_batch):
    n = h2.shape[0]
    stacked_specs = [pl.BlockSpec((1,) + c.shape[1:], functools.partial(lambda nd, i: (layer,) + (0,) * nd, c.ndim - 1))
                     for c in lru_params]
    return pl.pallas_call(
        functools.partial(_inproj_kernel, n_batches=n_batches, rows_per_batch=rows_per_batch),
        grid=(n // tm,),
        in_specs=[
            pl.BlockSpec((tm, D_MODEL), lambda i: (i, 0)),
            pl.BlockSpec((1, 1, D_MODEL), lambda i: (layer, 0, 0)),
            pl.BlockSpec((1, D_MODEL, W_ALL), lambda i: (layer, 0, 0)),
        ] + stacked_specs,
        out_specs=[
            pl.BlockSpec((tm, QKV_W), lambda i: (i, 0)),
            pl.BlockSpec((tm, REST_W), lambda i: (i, 0)),
            pl.BlockSpec((tm, LRU_OUT_W), lambda i: (i, 0)),
        ],
        out_shape=[
            jax.ShapeDtypeStruct((n, QKV_W), BF16),
            jax.ShapeDtypeStruct((n, REST_W), F32),
            jax.ShapeDtypeStruct((n, LRU_OUT_W), F32),
        ],
        scratch_shapes=[pltpu.VMEM((SUBLANES, LRU_W), F32)],
        compiler_params=pltpu.CompilerParams(
            dimension_semantics=("arbitrary",), vmem_limit_bytes=VMEM_LIMIT),
        name="inproj",
    )(h2, g_all, w_all, *lru_params)


def _level_reference(b, j, sub):
    T = b.shape[0]
    half = 1 << (j - 1)
    parts = []
    if j > 3:
        for blk in range(T >> j):
            r = (blk << j) + half - 1
            parts.append(jnp.broadcast_to(b[r:r + 1, :], (1 << j, b.shape[1])))
        return jnp.concatenate(parts, axis=0)
    for g in range(T // SUBLANES):
        x = b[g * SUBLANES:(g + 1) * SUBLANES, :]
        if j == 3:
            parts.append(jnp.broadcast_to(x[half - 1:half, :], x.shape))
            continue
        ref = x
        for off in range(-(half - 1), half + 1):
            if off != 0:
                pick = (sub & ((1 << j) - 1)) == (half - 1 + off)
                ref = jnp.where(pick, pltpu.roll(x, off % SUBLANES, 0), ref)
        parts.append(ref)
    return jnp.concatenate(parts, axis=0)


def _seqmix_kernel(gqk_ref, gv_ref, gg_ref, sm_ref, a_ref, u_ref, sl_ref,
                   mall_ref, lv_ref, wa2_ref, ba_ref, bf_ref, gng_ref,
                   ygla_ref, ylru_ref, ccol_ref, crow_ref,
                   st_ref, cc_ref, hc_ref):
    t = pl.program_id(1)
    T = T_SEQ

    @pl.when(t == 0)
    def _():
        st_ref[...] = jnp.zeros_like(st_ref)
        cc_ref[...] = jnp.zeros_like(cc_ref)
        hc_ref[...] = jnp.zeros_like(hc_ref)

    row = lax.broadcasted_iota(jnp.int32, (T, 1), 0)
    valid = ((t * T + row) >= (PAD - N_META)).astype(F32)
    tri = mall_ref[...]

    groups = T // SUBLANES
    sub = lax.broadcasted_iota(jnp.int32, (SUBLANES, 1), 0)

    def lru_stages():
        h_prev = hc_ref[...]
        for g in range(groups):
            rows = slice(g * SUBLANES, (g + 1) * SUBLANES)
            ag, ug = a_ref[0, rows, :], u_ref[0, rows, :]
            d = 1
            while d < SUBLANES:
                keep = sub >= d
                a_sh = jnp.where(keep, pltpu.roll(ag, d, 0), 1.0)
                u_sh = jnp.where(keep, pltpu.roll(ug, d, 0), 0.0)
                ug = ag * u_sh + ug
                ag = ag * a_sh
                d *= 2
            hg = ag * h_prev + ug
            h_prev = hg[SUBLANES - 1:SUBLANES, :]
            ylru_ref[0, rows, :] = (hg * sl_ref[0, rows, :] * valid[rows, :]).astype(BF16)
            if g % 2 == 1:
                yield
        hc_ref[...] = h_prev

    def gla_stages():
        sm = sm_ref[0]
        log_f = _log_sigmoid(sm + bf_ref[0]) * valid
        z = _dot(sm.astype(BF16), wa2_ref[0]) + ba_ref[0]
        log_a = _log_sigmoid(z) / GLA_GATE_NORM
        yield
        c = _dot3(tri, log_f) + cc_ref[...]
        cc_ref[...] = c[T - 1:T, :]
        c2 = c * LOG2E
        ccol_ref[0] = c2
        c_keys = jnp.where(valid > 0.0, c2, BIG)
        crow_ref[0] = c_keys.T[0:FOX_HEADS, :]
        b = _dot3(tri, log_a)
        yield
        gqk = gqk_ref[0]
        q = gqk[:, 0:GLA_KW]
        k = gqk[:, GLA_KW:2 * GLA_KW] * valid
        v = gv_ref[0].astype(BF16)
        lane_k = lax.broadcasted_iota(jnp.int32, (1, GLA_KW), 1)
        head_masks = [(lane_k >= hh * GLA_DK) & (lane_k < (hh + 1) * GLA_DK) for hh in range(GLA_HEADS)]

        def stack_heads(x):
            return jnp.concatenate([jnp.where(mk, x, 0.0).astype(BF16) for mk in head_masks], axis=0)

        lv = lv_ref[...]
        a_st = jnp.where(lv == 0, _dot_nt(stack_heads(q), k.astype(BF16)), 0.0)
        yield
        for j in range(1, N_LEVELS + 1):
            bref = _level_reference(b, j, sub)
            e = jnp.exp2(jnp.abs(b - bref) * -LOG2E)
            a_st = a_st + jnp.where(lv == j, _dot_nt(stack_heads(q * e), (k * e).astype(BF16)), 0.0)
            yield
        a_st = a_st.astype(BF16)
        b_last = b[T - 1:T, :]
        qd_st = stack_heads(q * jnp.exp(b))
        kd = (k * jnp.exp(b_last - b)).astype(BF16)
        st = st_ref[...]
        st_bf = st.astype(BF16)
        st_new = st * jnp.exp(b_last)
        gg = gg_ref[0]
        for hh in range(GLA_HEADS):
            vh = v[:, hh * GLA_DV:(hh + 1) * GLA_DV]
            o = _dot(a_st[hh * T:(hh + 1) * T], vh) + _dot_nt(qd_st[hh * T:(hh + 1) * T], st_bf)
            o = o * lax.rsqrt(jnp.mean(o * o, axis=-1, keepdims=True) + EPS)
            o = o * gng_ref[0, :, hh * GLA_DV:(hh + 1) * GLA_DV]
            y = o * _silu(gg[:, hh * GLA_DV:(hh + 1) * GLA_DV]) * valid
            ygla_ref[0, :, hh * GLA_DV:(hh + 1) * GLA_DV] = y.astype(BF16)
            st_new = st_new + jnp.where(head_masks[hh], _dot_tn(vh, kd), 0.0)
            yield
        st_ref[...] = st_new

    streams = [gla_stages(), lru_stages()]
    while streams:
        for gen in list(streams):
            if next(gen, StopIteration) is StopIteration:
                streams.remove(gen)


def _seqmix(rest3, lru3, shared, stacked, layer, B, L):
    T = T_SEQ
    nt = L // T
    consts = list(shared) + list(stacked)
    const_specs = [pl.BlockSpec(c.shape, functools.partial(lambda nd, b, t: (0,) * nd, c.ndim)) for c in shared]
    const_specs += [pl.BlockSpec((1,) + c.shape[1:], functools.partial(lambda nd, b, t: (layer,) + (0,) * nd, c.ndim - 1))
                    for c in stacked]
    return pl.pallas_call(
        _seqmix_kernel,
        grid=(B, nt),
        in_specs=[
            pl.BlockSpec((1, T, 2 * GLA_KW), lambda b, t: (b, t, R_GQ // (2 * GLA_KW))),
            pl.BlockSpec((1, T, GLA_W), lambda b, t: (b, t, R_GV // GLA_W)),
            pl.BlockSpec((1, T, GLA_W), lambda b, t: (b, t, R_GG // GLA_W)),
            pl.BlockSpec((1, T, LANES), lambda b, t: (b, t, R_SM // LANES)),
            pl.BlockSpec((1, T, LRU_W), lambda b, t: (b, t, 0)),
            pl.BlockSpec((1, T, LRU_W), lambda b, t: (b, t, 1)),
            pl.BlockSpec((1, T, LRU_W), lambda b, t: (b, t, 2)),
        ] + const_specs,
        out_specs=[
            pl.BlockSpec((1, T, GLA_W), lambda b, t: (b, t, 0)),
            pl.BlockSpec((1, T, LRU_W), lambda b, t: (b, t, 0)),
            pl.BlockSpec((1, T, LANES), lambda b, t: (b, t, 0)),
            pl.BlockSpec((1, FOX_HEADS, T), lambda b, t: (b, 0, t)),
        ],
        out_shape=[
            jax.ShapeDtypeStruct((B, L, GLA_W), BF16),
            jax.ShapeDtypeStruct((B, L, LRU_W), BF16),
            jax.ShapeDtypeStruct((B, L, LANES), F32),
            jax.ShapeDtypeStruct((B, FOX_HEADS, L), F32),
        ],
        scratch_shapes=[
            pltpu.VMEM((GLA_DV, GLA_KW), F32),
            pltpu.VMEM((1, LANES), F32),
            pltpu.VMEM((1, LRU_W), F32),
        ],
        compiler_params=pltpu.CompilerParams(
            dimension_semantics=("arbitrary", "arbitrary"), vmem_limit_bytes=VMEM_LIMIT),
        name="seqmix",
    )(rest3, rest3, rest3, rest3, lru3, lru3, lru3, *consts)


def _head_sqnorm_max(x, first):
    x2 = x.astype(F32)
    x2 = x2 * x2
    n0 = jnp.sum(jnp.where(first, x2, 0.0), axis=-1, keepdims=True)
    n1 = jnp.sum(jnp.where(first, 0.0, x2), axis=-1, keepdims=True)
    return jnp.max(n0, axis=0, keepdims=True), jnp.max(n1, axis=0, keepdims=True)


def _fox_kernel(q_ref, k_ref, v_ref, ccol_ref, crow_ref, fg_ref, bias_ref, o_ref, kst_ref, sa_ref, sb_ref, m_ref, acc_ref):
    hp = pl.program_id(1)
    i = pl.program_id(2)
    nkb = k_ref.shape[1] // TK
    lane = lax.broadcasted_iota(jnp.int32, (1, LANES), 1)
    first = lane < FOX_DH

    @pl.when(i == 0)
    def _():
        kst_ref[...] = jnp.zeros_like(kst_ref)
        for j in range(nkb):
            n0, n1 = _head_sqnorm_max(k_ref[0, j * TK:(j + 1) * TK, :], first)
            cmin = jnp.min(crow_ref[0, 0, :, j * TK:(j + 1) * TK], axis=-1, keepdims=True)
            kst_ref[j:j + 1, :] = jnp.where(lane == 0, n0, jnp.where(lane == 1, n1, jnp.where(
                lane == 2, cmin[0:1], jnp.where(lane == 3, cmin[1:2], 0.0))))

    q = q_ref[0]
    zero = jnp.zeros_like(q)
    qh = (jnp.where(first, q, zero), jnp.where(first, zero, q))
    ccol = ccol_ref[0]
    cq = tuple(jnp.sum(jnp.where(lane == 2 * hp + hh, ccol, 0.0), axis=-1, keepdims=True) for hh in range(2))
    lane2 = lax.broadcasted_iota(jnp.int32, (1, 2 * LANES), 1)
    sel0 = (lane2 < FOX_DH) | (lane2 == LANES)
    ones_blk = jnp.broadcast_to((lane < 2).astype(BF16), (TK, LANES))

    def scores(j, dst):
        off = pl.multiple_of(j * TK, LANES)
        k = k_ref[0, pl.ds(off, TK), :]
        ck = crow_ref[0, 0, :, pl.ds(off, TK)]
        for hh in range(2):
            dst[hh] = _dot_nt(qh[hh], k) + cq[hh] - ck[hh:hh + 1, :]

    def consume(j, src, diag):
        off = pl.multiple_of(j * TK, LANES)
        vext = jnp.concatenate([v_ref[0, pl.ds(off, TK), :], ones_blk], axis=1)
        half = TQ // 2
        parts = ((0, half, half), (half, TQ, TK)) if diag else ((0, TQ, TK),)
        for r0, r1, nk in parts:
            alphas, pvs = [], []
            for hh in range(2):
                s = src[hh, r0:r1, 0:nk]
                if diag:
                    s = s + bias_ref[r0:r1, 0:nk]
                m_old = m_ref[hh, r0:r1, :]
                m_new = jnp.maximum(m_old, jnp.max(s, axis=-1, keepdims=True))
                m_ref[hh, r0:r1, :] = m_new
                alphas.append(jnp.exp2(m_old - m_new))
                pvs.append(_dot(jnp.exp2(s - m_new).astype(BF16), vext[0:nk, :]))
            acc_ref[r0:r1, :] = (acc_ref[r0:r1, :] * jnp.where(sel0, alphas[0], alphas[1])
                                 + jnp.where(sel0, pvs[0], pvs[1]))

    def scores_both(j, dst):
        off = pl.multiple_of(j * TK, LANES)
        k = k_ref[0, pl.ds(off, TK), :]
        kz = jnp.zeros_like(k)
        k2 = jnp.concatenate([jnp.where(first, k, kz), jnp.where(first, kz, k)], axis=0)
        ck = crow_ref[0, 0, :, pl.ds(off, TK)]
        s2 = _dot_nt(q, k2)
        for hh in range(2):
            dst[hh] = s2[:, hh * TK:(hh + 1) * TK] + cq[hh] - ck[hh:hh + 1, :]

    def consume_both(j, src):
        off = pl.multiple_of(j * TK, LANES)
        v = v_ref[0, pl.ds(off, TK), :]
        vz = jnp.zeros_like(v)
        v2 = jnp.concatenate([
            jnp.concatenate([jnp.where(first, v, vz), jnp.broadcast_to((lane == 0).astype(BF16), (TK, LANES))], axis=1),
            jnp.concatenate([jnp.where(first, vz, v), jnp.broadcast_to((lane == 1).astype(BF16), (TK, LANES))], axis=1),
        ], axis=0)
        alphas, ps = [], []
        for hh in range(2):
            s = src[hh]
            m_old = m_ref[hh]
            m_new = jnp.maximum(m_old, jnp.max(s, axis=-1, keepdims=True))
            m_ref[hh] = m_new
            alphas.append(jnp.exp2(m_old - m_new))
            ps.append(jnp.exp2(s - m_new).astype(BF16))
        pv = _dot(jnp.concatenate(ps, axis=1), v2)
        acc_ref[...] = acc_ref[...] * jnp.where(sel0, alphas[0], alphas[1]) + pv

    m_ref[...] = jnp.full(m_ref.shape, -BIG, F32)
    acc_ref[...] = jnp.zeros(acc_ref.shape, F32)
    scores(i, sa_ref)
    scores(jnp.maximum(i - 1, 0), sb_ref)
    consume(i, sa_ref, True)

    qn = _head_sqnorm_max(q, first)
    kst = kst_ref[...]
    rowj = lax.broadcasted_iota(jnp.int32, (kst.shape[0], 1), 0)
    starts = []
    for hh in range(2):
        kn = kst[:, hh:hh + 1] * 1.01
        cmin = kst[:, 2 + hh:3 + hh]
        m_low = jnp.min(m_ref[hh], axis=0, keepdims=True)
        ub = jnp.max(cq[hh], axis=0, keepdims=True) - cmin + jnp.sqrt(qn[hh] * kn) - m_low
        skippable = (ub <= -SKIP_LOG2) & (rowj < i)
        starts.append(jnp.min(jnp.where(skippable, kst.shape[0], rowj)))
    j0 = jnp.minimum(starts[0], starts[1])
    nb = i - j0

    def blk(kk):
        return jnp.where(kk == 0, i - 1, j0 + kk - 1)

    def pair(n, carry):
        scores_both(blk(2 * n + 1), sa_ref)
        consume_both(blk(2 * n), sb_ref)
        scores_both(blk(2 * n + 2), sb_ref)
        consume_both(blk(2 * n + 1), sa_ref)
        return carry

    npairs = jnp.maximum(nb - 1, 0) // 2
    lax.fori_loop(0, npairs, pair, 0)

    @pl.when((nb > 0) & (nb % 2 == 0))
    def _():
        scores_both(blk(nb - 1), sa_ref)
        consume_both(blk(nb - 2), sb_ref)
        consume_both(blk(nb - 1), sa_ref)

    @pl.when(nb % 2 == 1)
    def _():
        consume_both(blk(nb - 1), sb_ref)

    acc = acc_ref[...]
    o = acc[:, 0:LANES] / jnp.where(first, acc[:, LANES:LANES + 1], acc[:, LANES + 1:LANES + 2])
    row = lax.broadcasted_iota(jnp.int32, (TQ, 1), 0)
    valid = ((i * TQ + row) >= (PAD - N_META)).astype(F32)
    o_ref[0] = (o * _silu(fg_ref[0]) * valid).astype(BF16)


def _fox(qkv3, ccol, crow4, rest3, B, L):
    nq = L // TQ
    causal_bias = jnp.where(jnp.arange(TK)[None, :] <= jnp.arange(TQ)[:, None], 0.0, -BIG).astype(F32)
    hpairs = FOX_HEADS // 2
    return pl.pallas_call(
        _fox_kernel,
        grid=(B, hpairs, nq),
        in_specs=[
            pl.BlockSpec((1, TQ, LANES), lambda b, h, i: (b, i, h)),
            pl.BlockSpec((1, L, LANES), lambda b, h, i: (b, 0, FOX_W // LANES + h)),
            pl.BlockSpec((1, L, LANES), lambda b, h, i: (b, 0, 2 * FOX_W // LANES + h)),
            pl.BlockSpec((1, TQ, LANES), lambda b, h, i: (b, i, 0)),
            pl.BlockSpec((1, 1, 2, L), lambda b, h, i: (b, h, 0, 0)),
            pl.BlockSpec((1, TQ, LANES), lambda b, h, i: (b, i, R_FG // LANES + h)),
            pl.BlockSpec((TQ, TK), lambda b, h, i: (0, 0)),
        ],
        out_specs=pl.BlockSpec((1, TQ, LANES), lambda b, h, i: (b, i, h)),
        out_shape=jax.ShapeDtypeStruct((B, L, FOX_W), BF16),
        scratch_shapes=[
            pltpu.VMEM((2 * SUBLANES, LANES), F32),
            pltpu.VMEM((2, TQ, TK), F32),
            pltpu.VMEM((2, TQ, TK), F32),
            pltpu.VMEM((2, TQ, 1), F32),
            pltpu.VMEM((TQ, 2 * LANES), F32),
        ],
        compiler_params=pltpu.CompilerParams(
            dimension_semantics=("arbitrary", "arbitrary", "arbitrary"), vmem_limit_bytes=VMEM_LIMIT),
        name="fox",
    )(qkv3, qkv3, qkv3, ccol, crow4, rest3, causal_bias)


def _outproj_math(yf, yg, yl, h, w_ref, g_ref):
    acc = _dot(yf, w_ref[0, 0:FOX_W, :])
    acc = acc + _dot(yg, w_ref[0, FOX_W:FOX_W + GLA_W, :])
    acc = acc + _dot(yl, w_ref[0, FOX_W + GLA_W:D_MIX, :])
    ms = jnp.mean(acc * acc, axis=-1, keepdims=True)
    return h + acc * lax.rsqrt(ms + EPS) * g_ref[0]


def _outproj_kernel(yf_ref, yg_ref, yl_ref, h_ref, w_ref, g_ref, o_ref):
    o_ref[...] = _outproj_math(yf_ref[...], yg_ref[...], yl_ref[...], h_ref[...], w_ref, g_ref)


def _outproj(yf, yg, yl, h2, w_all, g_all, layer, tm):
    n = h2.shape[0]
    return pl.pallas_call(
        _outproj_kernel,
        grid=(n // tm,),
        in_specs=[
            pl.BlockSpec((tm, FOX_W), lambda i: (i, 0)),
            pl.BlockSpec((tm, GLA_W), lambda i: (i, 0)),
            pl.BlockSpec((tm, LRU_W), lambda i: (i, 0)),
            pl.BlockSpec((tm, D_MODEL), lambda i: (i, 0)),
            pl.BlockSpec((1, D_MIX, D_MODEL), lambda i: (layer, 0, 0)),
            pl.BlockSpec((1, 1, D_MODEL), lambda i: (layer, 0, 0)),
        ],
        out_specs=pl.BlockSpec((tm, D_MODEL), lambda i: (i, 0)),
        out_shape=jax.ShapeDtypeStruct((n, D_MODEL), F32),
        compiler_params=pltpu.CompilerParams(
            dimension_semantics=("arbitrary",), vmem_limit_bytes=VMEM_LIMIT),
        name="outproj",
    )(yf, yg, yl, h2, w_all, g_all)


def _outproj_last_kernel(*refs):
    n = OUT_PARTS
    yf, yg, yl, h = [jnp.concatenate([r[0] for r in refs[a * n:(a + 1) * n]], axis=0) for a in range(4)]
    w_ref, g_ref, o_ref = refs[4 * n:]
    o_ref[0] = _outproj_math(yf, yg, yl, h, w_ref, g_ref)


def _outproj_last(yf3, yg3, yl3, h3, w_all, g_all, layer, S):
    B = h3.shape[0]
    tm = OUT_PARTS * PAD

    def part_specs(width):
        return [pl.BlockSpec((1, PAD, width), functools.partial(lambda r, b, j: (b, 1 + OUT_PARTS * j + r, 0), r))
                for r in range(OUT_PARTS)]

    in_specs = part_specs(FOX_W) + part_specs(GLA_W) + part_specs(LRU_W) + part_specs(D_MODEL) + [
        pl.BlockSpec((1, D_MIX, D_MODEL), lambda b, j: (layer, 0, 0)),
        pl.BlockSpec((1, 1, D_MODEL), lambda b, j: (layer, 0, 0)),
    ]
    args = [yf3] * OUT_PARTS + [yg3] * OUT_PARTS + [yl3] * OUT_PARTS + [h3] * OUT_PARTS + [w_all, g_all]
    return pl.pallas_call(
        _outproj_last_kernel,
        grid=(B, S // tm),
        in_specs=in_specs,
        out_specs=pl.BlockSpec((1, tm, D_MODEL), lambda b, j: (b, j, 0)),
        out_shape=jax.ShapeDtypeStruct((B, S, D_MODEL), F32),
        compiler_params=pltpu.CompilerParams(
            dimension_semantics=("arbitrary", "arbitrary"), vmem_limit_bytes=VMEM_LIMIT),
        name="outproj_last",
    )(*args)


def _level_constants():
    T = T_SEQ
    t = jnp.arange(T)[:, None]
    s = jnp.arange(T)[None, :]
    mall = (s <= t).astype(BF16)
    x = t ^ s
    level = jnp.zeros((T, T), jnp.int32)
    for j in range(1, N_LEVELS + 1):
        level = jnp.where((x >> (j - 1)) == 1, j, level)
    level = jnp.where(s > t, -1, level)
    return mall, jnp.tile(level, (GLA_HEADS, 1))


def _prepare_weights(w_in, w_out, b_f):
    depth = w_in.shape[0]
    sizes = (FOX_W, FOX_W, FOX_W, FOX_HEADS, FOX_W, GLA_KW, GLA_KW, GLA_W, GLA_RANK, GLA_W, LRU_W, LRU_W)
    names = ("fq", "fk", "fv", "ff", "fg", "gq", "gk", "gv", "ga", "gg", "lx", "lg")
    offs, acc = {}, 0
    for nm, sz in zip(names, sizes):
        offs[nm] = acc
        acc += sz
    head_order = jnp.argsort(b_f, axis=-1)

    def head_cols(nm, width):
        return (offs[nm] + head_order[:, :, None] * width + jnp.arange(width)[None, None, :]).reshape(depth, -1)

    def plain_cols(nm, width):
        return jnp.broadcast_to(offs[nm] + jnp.arange(width)[None, :], (depth, width))

    n_pad = LANES - FOX_HEADS - GLA_RANK
    cols = jnp.concatenate([
        head_cols("fq", FOX_DH), head_cols("fk", FOX_DH), head_cols("fv", FOX_DH),
        head_cols("fg", FOX_DH), plain_cols("gq", GLA_KW), plain_cols("gk", GLA_KW), plain_cols("gv", GLA_W),
        plain_cols("gg", GLA_W), plain_cols("lx", LRU_W), plain_cols("lg", LRU_W),
        head_cols("ff", 1), plain_cols("ga", GLA_RANK), jnp.zeros((depth, n_pad), jnp.int32)], axis=1)
    scale = jnp.concatenate([
        jnp.full((FOX_W,), FOX_DH ** -0.5 * LOG2E, F32), jnp.ones((2 * FOX_W + FOX_W,), F32),
        jnp.full((GLA_KW,), GLA_DK ** -0.5, F32),
        jnp.ones((GLA_KW + 2 * GLA_W + 2 * LRU_W + FOX_HEADS + GLA_RANK,), F32), jnp.zeros((n_pad,), F32)])
    w_all = (jnp.take_along_axis(w_in, cols[:, None, :], axis=2) * scale).astype(BF16)
    rows = jnp.concatenate([
        (head_order[:, :, None] * FOX_DH + jnp.arange(FOX_DH)[None, None, :]).reshape(depth, -1),
        jnp.broadcast_to(jnp.arange(FOX_W, D_MIX)[None, :], (depth, D_MIX - FOX_W))], axis=1)
    wo_all = jnp.take_along_axis(w_out, rows[:, :, None], axis=1).astype(BF16)
    return w_all, wo_all, head_order


def kernel(x, meta, pre_g, w_in, b_f, w_a2, b_a, gla_norm_g, conv_w, conv_b, w_r, b_r, w_i, b_i, lru_lambda, w_out, post_g):
    B, S, D = x.shape
    L = S + PAD
    tm_in = 512
    tm_out = 512 if (B * L) % 512 == 0 else 256
    assert D == D_MODEL and L % T_SEQ == 0 and L % TQ == 0 and L // TK <= 2 * SUBLANES
    assert (B * L) % tm_in == 0 and S % (OUT_PARTS * PAD) == 0
    depth = w_in.shape[0]
    dt = x.dtype
    h = jnp.concatenate([jnp.zeros((B, PAD - N_META, D), dt),
                         jnp.broadcast_to(meta.astype(dt)[None], (B, N_META, D)), x], axis=1)
    h2 = h.reshape(B * L, D)

    w_all, wo_all, head_order = _prepare_weights(w_in, w_out, b_f)
    row3 = lambda a: a[:, None, :]
    wa2 = jnp.zeros((depth, LANES, GLA_KW), F32).at[:, FOX_HEADS:FOX_HEADS + GLA_RANK].set(w_a2).astype(BF16)
    bfp = jnp.zeros((depth, 1, LANES), F32).at[:, 0, 0:FOX_HEADS].set(jnp.take_along_axis(b_f, head_order, axis=1))
    wri = jnp.concatenate([w_r, w_i], axis=-1).astype(BF16)
    shared = list(_level_constants())
    stacked = [wa2, row3(b_a), bfp, row3(gla_norm_g)]
    lru_params = [conv_w, row3(conv_b), wri, row3(b_r), row3(b_i), row3(lru_lambda)]
    pre_g3, post_g3 = row3(pre_g), row3(post_g)

    out = None
    for l in range(depth):
        qkv, rest, lru = _inproj(h2, pre_g3, w_all, lru_params, l, tm_in, B, L)
        rest3 = rest.reshape(B, L, REST_W)
        ygla, ylru, ccol, crow = _seqmix(rest3, lru.reshape(B, L, LRU_OUT_W), shared, stacked, l, B, L)
        yfox = _fox(qkv.reshape(B, L, QKV_W), ccol, crow.reshape(B, FOX_HEADS // 2, 2, L), rest3, B, L)
        if l + 1 < depth:
            h2 = _outproj(yfox.reshape(B * L, FOX_W), ygla.reshape(B * L, GLA_W), ylru.reshape(B * L, LRU_W),
                          h2, wo_all, post_g3, l, tm_out)
        else:
            out = _outproj_last(yfox, ygla, ylru, h2.reshape(B, L, D), wo_all, post_g3, l, S)
    return out
```
